```python
import math
import jax, jax.numpy as jnp
from jax import lax
import numpy as np

D_MODEL = 1024
BATCH = 32
SEQ = 256
DEPTH = 4
DEC_BATCH = 4
DEC_SEQ = 4096
PAST_LEN = 256

GRID_W = 64
N_MIXERS = 2
EPS = 1e-6
GLA_HEADS = 4
GLA_DK = D_MODEL // 2
GLA_DV = D_MODEL
GLA_HK = GLA_DK // GLA_HEADS
GLA_HV = GLA_DV // GLA_HEADS
GLA_RANK = 16
GLA_GATE_NORM = 16.0
GLA_CHUNK = 64
GLA_IN = 2 * GLA_DK + 2 * GLA_DV + 2 * GLA_RANK
SSD_DI = 2 * D_MODEL
SSD_HEADDIM = 64
SSD_HEADS = SSD_DI // SSD_HEADDIM
SSD_GROUPS = 4
SSD_HPG = SSD_HEADS // SSD_GROUPS
SSD_STATE = 128
SSD_CONV_W = 5
SSD_CONV_CH = SSD_DI + 2 * SSD_GROUPS * SSD_STATE
SSD_IN = SSD_DI + SSD_CONV_CH + 2 * SSD_HEADS
SSD_CHUNK = 128
FFN_DIM = 2816
N_EXPERTS = 8
TOP_K = 2
EXPERT_DIM = 3584
MOE_BLOCK = 128
N_GLA = (DEPTH + 1) // 2
N_SSD = DEPTH // 2

kernel_name = 'hybrid_gla_ssd_diffusion_step'


def _rmsnorm(x, g):
    x32 = x.astype(jnp.float32)
    y = x32 * lax.rsqrt(jnp.mean(x32 * x32, axis=-1, keepdims=True) + EPS)
    return (y * g.astype(jnp.float32)).astype(x.dtype)


def _to_colmajor(x):
    bsz, t, d = x.shape
    rows = t // GRID_W
    return x.reshape(bsz, rows, GRID_W, d).transpose(0, 2, 1, 3).reshape(bsz, t, d)


def _to_rowmajor(x):
    bsz, t, d = x.shape
    rows = t // GRID_W
    return x.reshape(bsz, GRID_W, rows, d).transpose(0, 2, 1, 3).reshape(bsz, t, d)


def _gla_chunked(q, k, v, g, s0):
    bsz, t, nh, dk = q.shape
    dv = v.shape[-1]
    nc = t // GLA_CHUNK
    q = q.reshape(bsz, nc, GLA_CHUNK, nh, dk)
    k = k.reshape(bsz, nc, GLA_CHUNK, nh, dk)
    g = g.reshape(bsz, nc, GLA_CHUNK, nh, dk)
    v = v.reshape(bsz, nc, GLA_CHUNK, nh, dv)
    b = jnp.cumsum(g, axis=2)
    b_last = b[:, :, -1]
    qg = q * jnp.exp(b)
    kg = k * jnp.exp(-b)
    kd = k * jnp.exp(b_last[:, :, None] - b)
    mask = jnp.tril(jnp.ones((GLA_CHUNK, GLA_CHUNK), dtype=bool))
    att = jnp.where(mask, jnp.einsum('bnchk,bnshk->bnhcs', qg, kg), 0.0)
    o = jnp.einsum('bnhcs,bnshv->bnchv', att, v)
    ds = jnp.einsum('bnchk,bnchv->bnhkv', kd, v)

    def step(s, inp):
        dec, d = inp
        return dec[..., None] * s + d, s

    s_fin, s_start = lax.scan(step, s0, (jnp.moveaxis(jnp.exp(b_last), 1, 0), jnp.moveaxis(ds, 1, 0)))
    o = o + jnp.einsum('bnchk,nbhkv->bnchv', qg, s_start)
    return o.reshape(bsz, t, nh, dv), s_fin


def _gla_mixer(h, w_in, w_a2, b_a, norm_g, w_out, s0):
    bsz, t, _ = h.shape
    f32 = jnp.float32
    proj = h @ w_in
    q, k, v, r, a_lr = jnp.split(proj, [GLA_DK, 2 * GLA_DK, 2 * GLA_DK + GLA_DV, 2 * GLA_DK + 2 * GLA_DV], axis=-1)
    a_lr = a_lr.reshape(bsz, t, 2, GLA_RANK)
    log_a = jax.nn.log_sigmoid((jnp.einsum('btdr,drk->btdk', a_lr, w_a2) + b_a).astype(f32)) / GLA_GATE_NORM
    log_a = log_a.reshape(bsz, t, 2, GLA_HEADS, GLA_HK)
    q = q.reshape(bsz, t, GLA_HEADS, GLA_HK).astype(f32) * (GLA_HK ** -0.5)
    k = k.reshape(bsz, t, GLA_HEADS, GLA_HK).astype(f32)
    v = v.reshape(bsz, t, GLA_HEADS, GLA_HV).astype(f32)
    s0 = s0.astype(f32)
    o_f, s_f = _gla_chunked(q, k, v, log_a[:, :, 0], s0[:, 0])
    o_b, s_b = _gla_chunked(jnp.flip(q, 1), jnp.flip(k, 1), jnp.flip(v, 1), jnp.flip(log_a[:, :, 1], 1), s0[:, 1])
    o = o_f + jnp.flip(o_b, 1)
    o = _rmsnorm(o, norm_g).reshape(bsz, t, GLA_DV) * jax.nn.silu(r.astype(f32))
    y = o.astype(h.dtype) @ w_out
    return y, jnp.stack([s_f, s_b], axis=1)


def _dwconv_centred(x, w, b):
    ch = x.shape[-1]
    y = lax.conv_general_dilated(x, w[:, None, :].astype(x.dtype), window_strides=(1,),
                                 padding=[(SSD_CONV_W // 2, SSD_CONV_W // 2)],
                                 dimension_numbers=('NWC', 'WIO', 'NWC'), feature_group_count=ch)
    return y + b


def _ssd_chunked(x, dt, a, bm, cm, s0):
    bsz, t, ng, nr, hp = x.shape
    ns = bm.shape[-1]
    nz = t // SSD_CHUNK
    x = x.reshape(bsz, nz, SSD_CHUNK, ng, nr, hp)
    dt = dt.reshape(bsz, nz, SSD_CHUNK, ng, nr)
    bm = bm.reshape(bsz, nz, SSD_CHUNK, ng, ns)
    cm = cm.reshape(bsz, nz, SSD_CHUNK, ng, ns)
    cum = jnp.cumsum(dt * a, axis=2)
    cum_last = cum[:, :, -1]
    ct = jnp.moveaxis(cum, 2, -1)
    mask = jnp.tril(jnp.ones((SSD_CHUNK, SSD_CHUNK), dtype=bool))
    decay = jnp.exp(jnp.where(mask, ct[..., :, None] - ct[..., None, :], -jnp.inf))
    cb = jnp.einsum('bzcgn,bzsgn->bzgcs', cm, bm)
    w = decay * cb[:, :, :, None] * jnp.moveaxis(dt, 2, -1)[..., None, :]
    y = jnp.einsum('bzgrcs,bzsgrp->bzcgrp', w, x)
    ds = jnp.einsum('bzsgn,bzsgr,bzsgrp->bzgrnp', bm, jnp.exp(cum_last[:, :, None] - cum) * dt, x)

    def step(s, inp):
        dec, d = inp
        return dec[..., None, None] * s + d, s

    s_fin, s_start = lax.scan(step, s0, (jnp.moveaxis(jnp.exp(cum_last), 1, 0), jnp.moveaxis(ds, 1, 0)))
    y = y + jnp.einsum('bzcgn,bzcgr,zbgrnp->bzcgrp', cm, jnp.exp(cum), s_start)
    return y.reshape(bsz, t, ng, nr, hp), s_fin


def _ssd_mixer(h, w_in, conv_w, conv_b, dt_bias, a_log, d_skip, norm_g, w_out, s0):
    bsz, t, _ = h.shape
    f32 = jnp.float32
    proj = h @ w_in
    z, xbc, dt_raw = jnp.split(proj, [SSD_DI, SSD_DI + SSD_CONV_CH], axis=-1)
    xbc = jax.nn.silu(_dwconv_centred(xbc, conv_w, conv_b))
    xs, bm, cm = jnp.split(xbc, [SSD_DI, SSD_DI + SSD_GROUPS * SSD_STATE], axis=-1)
    xs = xs.reshape(bsz, t, SSD_GROUPS, SSD_HPG, SSD_HEADDIM).astype(f32)
    bm = bm.reshape(bsz, t, SSD_GROUPS, SSD_STATE).astype(f32)
    cm = cm.reshape(bsz, t, SSD_GROUPS, SSD_STATE).astype(f32)
    dt = jax.nn.softplus(dt_raw.reshape(bsz, t, 2, SSD_HEADS).astype(f32) + dt_bias.astype(f32))
    dt = dt.reshape(bsz, t, 2, SSD_GROUPS, SSD_HPG)
    a = -jnp.exp(a_log.astype(f32)).reshape(2, SSD_GROUPS, SSD_HPG)
    s0 = s0.astype(f32).reshape(bsz, 2, SSD_GROUPS, SSD_HPG, SSD_STATE, SSD_HEADDIM)
    y_f, s_f = _ssd_chunked(xs, dt[:, :, 0], a[0], bm, cm, s0[:, 0])
    y_b, s_b = _ssd_chunked(jnp.flip(xs, 1), jnp.flip(dt[:, :, 1], 1), a[1], jnp.flip(bm, 1), jnp.flip(cm, 1), s0[:, 1])
    y = y_f + jnp.flip(y_b, 1) + d_skip.astype(f32).reshape(SSD_GROUPS, SSD_HPG)[..., None] * xs
    y = y.reshape(bsz, t, SSD_DI) * jax.nn.silu(z.astype(f32))
    y = _rmsnorm(y.reshape(bsz, t, SSD_GROUPS, SSD_DI // SSD_GROUPS),
                 norm_g.reshape(SSD_GROUPS, SSD_DI // SSD_GROUPS)).reshape(bsz, t, SSD_DI)
    out = y.astype(h.dtype) @ w_out
    s_new = jnp.stack([s_f, s_b], axis=1).reshape(bsz, 2, SSD_HEADS, SSD_STATE, SSD_HEADDIM)
    return out, s_new


def _swiglu(h, w_gate, w_up, w_down):
    return (jax.nn.silu(h @ w_gate) * (h @ w_up)) @ w_down


def _moe_swiglu(x, w_router, w_gate, w_up, w_down):
    n_tok, d = x.shape
    logits = (x @ w_router).astype(jnp.float32)
    top_v, top_i = lax.top_k(logits, TOP_K)
    gates = jax.nn.softmax(top_v, axis=-1)
    n_asg = n_tok * TOP_K
    e_flat = top_i.reshape(-1)
    t_flat = jnp.repeat(jnp.arange(n_tok, dtype=jnp.int32), TOP_K)
    g_flat = gates.reshape(-1)
    order = jnp.argsort(e_flat)
    e_sorted = e_flat[order]
    counts = jnp.bincount(e_flat, length=N_EXPERTS)
    starts = jnp.cumsum(counts) - counts
    padded = (counts + MOE_BLOCK - 1) // MOE_BLOCK * MOE_BLOCK
    pad_ends = jnp.cumsum(padded)
    pad_starts = pad_ends - padded
    dest = pad_starts[e_sorted] + (jnp.arange(n_asg) - starts[e_sorted])
    n_blocks = n_asg // MOE_BLOCK + N_EXPERTS
    tok = jnp.zeros((n_blocks * MOE_BLOCK,), jnp.int32).at[dest].set(t_flat[order])
    wt = jnp.zeros((n_blocks * MOE_BLOCK,), jnp.float32).at[dest].set(g_flat[order])
    blk_e = jnp.minimum(jnp.searchsorted(pad_ends, jnp.arange(n_blocks) * MOE_BLOCK, side='right'), N_EXPERTS - 1)
    xb = x[tok].reshape(n_blocks, MOE_BLOCK, d)

    def expert_block(args):
        xe, e = args
        hdn = jax.nn.silu(xe @ w_gate[e]) * (xe @ w_up[e])
        return hdn @ w_down[e]

    yb = lax.map(expert_block, (xb, blk_e)).reshape(-1, d)
    return jax.ops.segment_sum(yb * wt[:, None].astype(yb.dtype), tok, num_segments=n_tok)


def setup_inputs(seed: int = 0) -> dict:
    key = jax.random.key(seed)
    ks = jax.random.split(key, 32)
    nrm = jax.random.normal
    d = D_MODEL
    n_dense = N_GLA
    n_moe = N_SSD
    dt0 = jnp.exp(jax.random.uniform(ks[17], (N_SSD, 2, SSD_HEADS), minval=math.log(1e-3), maxval=math.log(1e-1)))
    return {
        'x_prompt': nrm(ks[0], (BATCH, SEQ, d)),
        'x_sample': nrm(ks[1], (DEC_BATCH, DEC_SEQ, d)),
        'state_gla': 0.5 * nrm(ks[2], (DEC_BATCH, N_GLA, 2, GLA_HEADS, GLA_HK, GLA_HV)),
        'state_ssd': 0.5 * nrm(ks[3], (DEC_BATCH, N_SSD, 2, SSD_HEADS, SSD_STATE, SSD_HEADDIM)),
        'c': nrm(ks[4], (DEC_BATCH, d)),
        'c_ctx': nrm(ks[5], (d,)),
        'w_mod': nrm(ks[6], (DEPTH, d, 6 * d)) * (0.5 * d ** -0.5),
        'b_mod': 0.02 * nrm(ks[7], (DEPTH, 6 * d)),
        'norm_g': 1.0 + 0.05 * nrm(ks[8], (DEPTH, 4, d)),
        'gla_w_in': nrm(ks[9], (N_GLA, d, GLA_IN)) * d ** -0.5,
        'gla_w_a2': nrm(ks[10], (N_GLA, 2, GLA_RANK, GLA_DK)) * GLA_RANK ** -0.5,
        'gla_b_a': 0.1 * nrm(ks[11], (N_GLA, 2, GLA_DK)),
        'gla_norm_g': 1.0 + 0.05 * nrm(ks[12], (N_GLA, GLA_HV)),
        'gla_w_out': nrm(ks[13], (N_GLA, GLA_DV, d)) * GLA_DV ** -0.5,
        'ssd_w_in': nrm(ks[14], (N_SSD, d, SSD_IN)) * d ** -0.5,
        'ssd_conv_w': nrm(ks[15], (N_SSD, SSD_CONV_W, SSD_CONV_CH)) * SSD_CONV_W ** -0.5,
        'ssd_conv_b': 0.02 * nrm(ks[16], (N_SSD, SSD_CONV_CH)),
        'ssd_dt_bias': dt0 + jnp.log(-jnp.expm1(-dt0)),
        'ssd_a_log': jnp.log(jax.random.uniform(ks[18], (N_SSD, 2, SSD_HEADS), minval=1.0, maxval=16.0)),
        'ssd_d': 1.0 + 0.1 * nrm(ks[19], (N_SSD, SSD_HEADS)),
        'ssd_norm_g': 1.0 + 0.05 * nrm(ks[20], (N_SSD, SSD_DI)),
        'ssd_w_out': nrm(ks[21], (N_SSD, SSD_DI, d)) * SSD_DI ** -0.5,
        'ffn_w_gate': nrm(ks[22], (n_dense, d, FFN_DIM)) * d ** -0.5,
        'ffn_w_up': nrm(ks[23], (n_dense, d, FFN_DIM)) * d ** -0.5,
        'ffn_w_down': nrm(ks[24], (n_dense, FFN_DIM, d)) * FFN_DIM ** -0.5,
        'moe_w_router': nrm(ks[25], (n_moe, d, N_EXPERTS)) * d ** -0.5,
        'moe_w_gate': nrm(ks[26], (n_moe, N_EXPERTS, d, EXPERT_DIM)) * d ** -0.5,
        'moe_w_up': nrm(ks[27], (n_moe, N_EXPERTS, d, EXPERT_DIM)) * d ** -0.5,
        'moe_w_down': nrm(ks[28], (n_moe, N_EXPERTS, EXPERT_DIM, d)) * EXPERT_DIM ** -0.5,
    }


def reference(x_prompt, x_sample, state_gla, state_ssd, c, c_ctx, w_mod, b_mod, norm_g,
              gla_w_in, gla_w_a2, gla_b_a, gla_norm_g, gla_w_out,
              ssd_w_in, ssd_conv_w, ssd_conv_b, ssd_dt_bias, ssd_a_log, ssd_d, ssd_norm_g, ssd_w_out,
              ffn_w_gate, ffn_w_up, ffn_w_down, moe_w_router, moe_w_gate, moe_w_up, moe_w_down):
    f32 = jnp.float32

    def layer(l, x, mod, s0, latent):
        j = l // N_MIXERS
        sh1, sc1, g1, sh2, sc2, g2 = jnp.split(mod, 6, axis=-1)
        h = _rmsnorm(x, norm_g[l, 0]) * (1.0 + sc1) + sh1
        if l % N_MIXERS == 0:
            mix, s_new = _gla_mixer(h, gla_w_in[j], gla_w_a2[j], gla_b_a[j], gla_norm_g[j], gla_w_out[j], s0)
        else:
            hh = _to_colmajor(h) if latent else h
            mix, s_new = _ssd_mixer(hh, ssd_w_in[j], ssd_conv_w[j], ssd_conv_b[j], ssd_dt_bias[j], ssd_a_log[j],
                                    ssd_d[j], ssd_norm_g[j], ssd_w_out[j], s0)
            if latent:
                mix = _to_rowmajor(mix)
        x = x + g1 * _rmsnorm(mix, norm_g[l, 1])
        h = _rmsnorm(x, norm_g[l, 2]) * (1.0 + sc2) + sh2
        if l % 2 == 0:
            ff = _swiglu(h, ffn_w_gate[j], ffn_w_up[j], ffn_w_down[j])
        else:
            ff = _moe_swiglu(h.reshape(-1, h.shape[-1]), moe_w_router[j], moe_w_gate[j], moe_w_up[j],
                             moe_w_down[j]).reshape(h.shape)
        x = x + g2 * _rmsnorm(ff, norm_g[l, 3])
        return x, s_new

    bp = x_prompt.shape[0]
    y_prompt = x_prompt
    gla_new = []
    ssd_new = []
    for l in range(DEPTH):
        mod = jax.nn.silu(c_ctx) @ w_mod[l] + b_mod[l]
        if l % N_MIXERS == 0:
            s0 = jnp.zeros((bp, 2, GLA_HEADS, GLA_HK, GLA_HV), f32)
            y_prompt, s_new = layer(l, y_prompt, mod, s0, False)
            gla_new.append(s_new)
        else:
            s0 = jnp.zeros((bp, 2, SSD_HEADS, SSD_STATE, SSD_HEADDIM), f32)
            y_prompt, s_new = layer(l, y_prompt, mod, s0, False)
            ssd_new.append(s_new)

    y_sample = x_sample
    for l in range(DEPTH):
        mod = (jax.nn.silu(c) @ w_mod[l] + b_mod[l])[:, None, :]
        s0 = state_gla[:, l // N_MIXERS] if l % N_MIXERS == 0 else state_ssd[:, l // N_MIXERS]
        y_sample, _ = layer(l, y_sample, mod, s0, True)

    new_state_gla = jnp.stack(gla_new, axis=1)
    new_state_ssd = jnp.stack(ssd_new, axis=1)
    return (y_prompt, y_sample, new_state_gla, new_state_ssd)
```

```python
import functools

import jax
import jax.numpy as jnp
from jax import lax
from jax.experimental import pallas as pl
from jax.experimental.pallas import tpu as pltpu

F32 = jnp.float32
BF16 = jnp.bfloat16

D = 1024
DEPTH = 4
EPS = 1e-6
GRID_W = 64
GLA_H = 4
GLA_HK = 128
GLA_HV = 256
GLA_DK = 512
GLA_DV = 1024
GLA_RANK = 16
SSD_DI = 2048
SSD_P = 64
SSD_NH = 32
SSD_G = 4
SSD_HPG = 8
SSD_N = 128
SSD_CW = 5
SSD_XBC = SSD_DI + 2 * SSD_G * SSD_N
FFN_DIM = 2816
N_EXP = 8
EXP_DIM = 3584
CHUNK = 128
LANES = 128
EXP_ROWS = 256
EXP_FCHUNK = 512
VMEM_LIMIT = 56 * 1024 * 1024


def _cp(n_axes=1, vmem=VMEM_LIMIT):
    return pltpu.CompilerParams(dimension_semantics=("arbitrary",) * n_axes, vmem_limit_bytes=vmem)


def _dot(a, b):
    return jnp.dot(a, b, preferred_element_type=F32)


def _dot_nt(a, b):
    return lax.dot_general(a, b, (((1,), (1,)), ((), ())), preferred_element_type=F32)


def _dot_tn(a, b):
    return lax.dot_general(a, b, (((0,), (0,)), ((), ())), preferred_element_type=F32)


def _rms(x, g):
    ms = jnp.mean(x * x, axis=-1, keepdims=True)
    return x * lax.rsqrt(ms + EPS) * g


def _silu(x):
    return x * jax.nn.sigmoid(x)


def _softplus(x):
    return jnp.maximum(x, 0.0) + jnp.log1p(jnp.exp(-jnp.abs(x)))


def _hi_lo(x):
    hi = x.astype(BF16)
    lo = (x - hi.astype(F32)).astype(BF16)
    return hi, lo


def _cols_to_rows(x, n):
    return jnp.concatenate([x[:, c * D:(c + 1) * D] for c in range(n)], axis=0)


def _rows_to_cols(x, n):
    r = x.shape[0] // n
    return jnp.concatenate([x[c * r:(c + 1) * r, :] for c in range(n)], axis=1)


def _mod_kernel(c_ref, w_ref, b_ref, o_ref):
    s = _silu(c_ref[...]).astype(BF16)
    o_ref[0] = _dot(s, w_ref[0].astype(BF16)) + b_ref[0]


def _modulation(c_all, w_mod, b_mod):
    tn = 1536
    return pl.pallas_call(
        _mod_kernel,
        grid=(DEPTH, 6 * D // tn),
        in_specs=[
            pl.BlockSpec((8, D), lambda l, n: (0, 0)),
            pl.BlockSpec((1, D, tn), lambda l, n: (l, 0, n)),
            pl.BlockSpec((1, 1, tn), lambda l, n: (l, 0, n)),
        ],
        out_specs=pl.BlockSpec((1, 8, tn), lambda l, n: (l, 0, n)),
        out_shape=jax.ShapeDtypeStruct((DEPTH, 8, 6 * D), F32),
        compiler_params=_cp(2),
        name="modulation",
    )(c_all, w_mod, b_mod.reshape(DEPTH, 1, 6 * D))


class _Part:
    def __init__(self, nseq, t, mod_base, mod_per_seq):
        self.nseq, self.t, self.m = nseq, t, nseq * t
        self.mod_base = mod_base
        self.rows_per_mod = t if mod_per_seq else nseq * t

    def mod_spec(self, tm):
        base, per = self.mod_base, self.rows_per_mod
        return pl.BlockSpec((1, 6, D), lambda i: (base + (i * tm) // per, 0, 0))


def _modulate(x, mod_ref, ng_ref, shift, scale):
    mod = mod_ref[0]
    return _rms(x, ng_ref[...]) * (1.0 + mod[scale:scale + 1]) + mod[shift:shift + 1]


GLA_SEGS = (GLA_DK, GLA_DK, GLA_DV, GLA_DV, LANES)


def _gla_in_kernel(x_ref, mod_ref, ng_ref, w_ref, q_ref, k_ref, v_ref, r_ref, a_ref):
    h = _modulate(x_ref[...], mod_ref, ng_ref, 0, 1).astype(BF16)
    off = 0
    for o_ref, n in zip((q_ref, k_ref, v_ref, r_ref, a_ref), GLA_SEGS):
        o_ref[...] = _dot(h, w_ref[:, off:off + n]).astype(o_ref.dtype)
        off += n


def _gla_in(part, x, mod, ng, w):
    tm = 512
    row = lambda i: (i, 0)
    const = lambda i: (0, 0)
    return pl.pallas_call(
        _gla_in_kernel,
        grid=(part.m // tm,),
        in_specs=[
            pl.BlockSpec((tm, D), row),
            part.mod_spec(tm),
            pl.BlockSpec((1, D), const),
            pl.BlockSpec((D, sum(GLA_SEGS)), const),
        ],
        out_specs=[pl.BlockSpec((tm, n), row) for n in GLA_SEGS],
        out_shape=[jax.ShapeDtypeStruct((part.m, n), F32) for n in GLA_SEGS],
        compiler_params=_cp(1),
        name="gla_in",
    )(x, mod, ng, w)


def _gla_scan_kernel(*refs, nchunks, has_s0):
    if has_s0:
        q_ref, k_ref, v_ref, a_ref, wa_ref, ba_ref, s0_ref, o_ref, st_scr = refs
        sfin_ref = None
    else:
        q_ref, k_ref, v_ref, a_ref, wa_ref, ba_ref, o_ref, sfin_ref, st_scr = refs
    c = CHUNK
    row = lax.broadcasted_iota(jnp.int32, (c, c), 0)
    col = lax.broadcasted_iota(jnp.int32, (c, c), 1)
    scale = GLA_HK ** -0.5
    for d in (0, 1):
        keep = (row >= col) if d == 0 else (row <= col)
        tri = jnp.where(keep, 1.0, 0.0).astype(BF16)
        wa = wa_ref[d].astype(BF16)
        ba = ba_ref[d]
        if has_s0:
            st_scr[...] = s0_ref[d].T
        else:
            st_scr[...] = jnp.zeros_like(st_scr)

        def body(i, carry, d=d, keep=keep, tri=tri, wa=wa, ba=ba):
            ci = i if d == 0 else nchunks - 1 - i
            sl = pl.ds(pl.multiple_of(ci * c, c), c)
            a = _dot(a_ref[sl, :].astype(BF16), wa) + ba
            g = (jnp.minimum(a, 0.0) - jnp.log1p(jnp.exp(-jnp.abs(a)))) * (1.0 / 16.0)
            g_hi, g_lo = _hi_lo(g)
            b = _dot(tri, g_hi) + _dot(tri, g_lo)
            b_last = b[c - 1:c, :] if d == 0 else b[0:1, :]
            k = k_ref[sl, :]
            qg = (q_ref[sl, :] * scale * jnp.exp(b)).astype(BF16)
            kg = (k * jnp.exp(-b)).astype(BF16)
            kd = (k * jnp.exp(b_last - b)).astype(BF16)
            v = v_ref[sl, :].astype(BF16)
            att = jnp.where(keep, _dot_nt(qg, kg), 0.0).astype(BF16)
            st = st_scr[...]
            o = _dot(att, v) + _dot_nt(qg, st.astype(BF16))
            if d == 0:
                o_ref[sl, :] = o
            else:
                o_ref[sl, :] += o
            st_scr[...] = st * jnp.exp(b_last) + _dot_tn(v, kd)
            return carry

        lax.fori_loop(0, nchunks, body, 0)
        if sfin_ref is not None:
            sfin_ref[d] = st_scr[...].T


def _gla_scan(part, q, k, v, a, wa, ba, s0):
    t = part.t
    has_s0 = s0 is not None
    seq_head = lambda b, h: (b, h)
    in_specs = [
        pl.BlockSpec((t, GLA_HK), seq_head),
        pl.BlockSpec((t, GLA_HK), seq_head),
        pl.BlockSpec((t, GLA_HV), seq_head),
        pl.BlockSpec((t, LANES), lambda b, h: (b, 0)),
        pl.BlockSpec((2, LANES, GLA_HK), lambda b, h: (0, 0, h)),
        pl.BlockSpec((2, 1, GLA_HK), lambda b, h: (0, 0, h)),
    ]
    args = [q, k, v, a, wa, ba]
    o_spec = pl.BlockSpec((t, GLA_HV), seq_head)
    o_shape = jax.ShapeDtypeStruct((part.m, GLA_DV), F32)
    if has_s0:
        in_specs.append(pl.BlockSpec((None, 2, None, GLA_HK, GLA_HV), lambda b, h: (b, 0, h, 0, 0)))
        args.append(s0)
        out_specs, out_shape = o_spec, o_shape
    else:
        out_specs = [o_spec, pl.BlockSpec((None, 2, None, GLA_HK, GLA_HV), lambda b, h: (b, 0, h, 0, 0))]
        out_shape = [o_shape, jax.ShapeDtypeStruct((part.nseq, 2, GLA_H, GLA_HK, GLA_HV), F32)]
    res = pl.pallas_call(
        functools.partial(_gla_scan_kernel, nchunks=t // CHUNK, has_s0=has_s0),
        grid=(part.nseq, GLA_H),
        in_specs=in_specs,
        out_specs=out_specs,
        out_shape=out_shape,
        scratch_shapes=[pltpu.VMEM((GLA_HV, GLA_HK), F32)],
        compiler_params=_cp(2),
        name="gla_scan",
    )(*args)
    return (res, None) if has_s0 else res


def _mix_out_tail(mix, x, mod_ref, ng1_ref, ng2_ref):
    mod = mod_ref[0]
    xn = x + mod[2:3] * _rms(mix, ng1_ref[...])
    h2 = _rms(xn, ng2_ref[...]) * (1.0 + mod[4:5]) + mod[3:4]
    return xn, h2


def _gla_out_kernel(o_ref, r_ref, x_ref, mod_ref, gng_ref, ng1_ref, ng2_ref, w_ref, xn_ref, h2_ref):
    o = o_ref[...]
    gng = gng_ref[...]
    on = jnp.concatenate([_rms(o[:, h * GLA_HV:(h + 1) * GLA_HV], gng) for h in range(GLA_H)], axis=1)
    on = (on * _silu(r_ref[...])).astype(BF16)
    xn, h2 = _mix_out_tail(_dot(on, w_ref[...]), x_ref[...], mod_ref, ng1_ref, ng2_ref)
    xn_ref[...] = xn
    h2_ref[...] = h2.astype(h2_ref.dtype)


def _gla_out(part, o, r, x, mod, gng, ng1, ng2, w):
    tm = 512
    row = lambda i: (i, 0)
    const = lambda i: (0, 0)
    return pl.pallas_call(
        _gla_out_kernel,
        grid=(part.m // tm,),
        in_specs=[
            pl.BlockSpec((tm, GLA_DV), row),
            pl.BlockSpec((tm, GLA_DV), row),
            pl.BlockSpec((tm, D), row),
            part.mod_spec(tm),
            pl.BlockSpec((1, GLA_HV), const),
            pl.BlockSpec((1, D), const),
            pl.BlockSpec((1, D), const),
            pl.BlockSpec((GLA_DV, D), const),
        ],
        out_specs=[pl.BlockSpec((tm, D), row), pl.BlockSpec((tm, D), row)],
        out_shape=[jax.ShapeDtypeStruct((part.m, D), F32), jax.ShapeDtypeStruct((part.m, D), BF16)],
        compiler_params=_cp(1),
        name="gla_out",
    )(o, r, x, mod, gng, ng1, ng2, w)


def _ffn_kernel(h_ref, x_ref, mod_ref, ng_ref, wg_ref, wu_ref, wd_ref, o_ref):
    h = h_ref[...]
    hd = (_silu(_dot(h, wg_ref[...])) * _dot(h, wu_ref[...])).astype(BF16)
    ff = _dot(hd, wd_ref[...])
    o_ref[...] = x_ref[...] + mod_ref[0][5:6] * _rms(ff, ng_ref[...])


def _ffn(part, h2, x, mod, ng, wg, wu, wd):
    tm = 256
    row = lambda i: (i, 0)
    const = lambda i: (0, 0)
    once = pl.Buffered(1)
    return pl.pallas_call(
        _ffn_kernel,
        grid=(part.m // tm,),
        in_specs=[
            pl.BlockSpec((tm, D), row),
            pl.BlockSpec((tm, D), row),
            part.mod_spec(tm),
            pl.BlockSpec((1, D), const),
            pl.BlockSpec((D, FFN_DIM), const, pipeline_mode=once),
            pl.BlockSpec((D, FFN_DIM), const, pipeline_mode=once),
            pl.BlockSpec((FFN_DIM, D), const, pipeline_mode=once),
        ],
        out_specs=pl.BlockSpec((tm, D), row),
        out_shape=jax.ShapeDtypeStruct((part.m, D), F32),
        compiler_params=_cp(1),
        name="ffn",
    )(h2, x, mod, ng, wg, wu, wd)


SSD_NCOL = 8


def _ssd_in_kernel(x_ref, xp_ref, xn_ref, mod_ref, ng_ref, wz_ref, wx_ref, wdt_ref, cw_ref, cb_ref,
                   z_ref, xc_ref, bc_ref, cc_ref, dt_ref, *, colmajor, blocks_per_seq, tm):
    i = pl.program_id(0)
    first = (i % blocks_per_seq) == 0
    last = (i % blocks_per_seq) == blocks_per_seq - 1
    x = x_ref[...]
    if colmajor:
        x = _cols_to_rows(x, SSD_NCOL)
    xcat = jnp.concatenate([xp_ref[...], x, xn_ref[...]], axis=0)
    h = _modulate(xcat, mod_ref, ng_ref, 0, 1).astype(BF16)
    hm = h[8:8 + tm]
    z_ref[...] = _dot(hm, wz_ref[...]).astype(z_ref.dtype)
    dt_ref[...] = _dot(hm, wdt_ref[...])
    rows = lax.broadcasted_iota(jnp.int32, (tm + 16, 1), 0)
    inside = jnp.logical_and(rows >= jnp.where(first, 8, 0), rows < jnp.where(last, tm + 8, tm + 16))
    cw = cw_ref[...]
    cb = cb_ref[...]
    wcol = 512
    for cc in range(SSD_XBC // wcol):
        cs = slice(cc * wcol, (cc + 1) * wcol)
        p = jnp.where(inside, _dot(h, wx_ref[:, cs]), 0.0)
        acc = cw[0:1, cs] * p[6:6 + tm]
        for j in range(1, SSD_CW):
            acc = acc + cw[j:j + 1, cs] * p[6 + j:6 + j + tm]
        y = _silu(acc + cb[:, cs])
        if cc < 4:
            xc_ref[:, cs] = y.astype(xc_ref.dtype)
        elif cc == 4:
            bc_ref[...] = y.astype(bc_ref.dtype)
        else:
            cc_ref[...] = y.astype(cc_ref.dtype)


def _ssd_in(part, x, mod, ng, wz, wx, wdt, cw, cb, colmajor):
    m = part.m
    const = lambda i: (0, 0)
    if colmajor:
        tm = SSD_NCOL * GRID_W
        per_b = GRID_W // SSD_NCOL
        xv = x.reshape(part.nseq * GRID_W, GRID_W * D)
        x_specs = [
            pl.BlockSpec((GRID_W, SSD_NCOL * D), lambda i: (i // per_b, i % per_b)),
            pl.BlockSpec((8, D), lambda i: ((i // per_b) * 8 + 7, jnp.maximum((i % per_b) * SSD_NCOL - 1, 0))),
            pl.BlockSpec((8, D), lambda i: ((i // per_b) * 8,
                                            jnp.minimum((i % per_b) * SSD_NCOL + SSD_NCOL, GRID_W - 1))),
        ]
        xs = (xv, xv, xv)
    else:
        tm = part.t
        nb8 = m // 8
        x_specs = [
            pl.BlockSpec((tm, D), lambda i: (i, 0)),
            pl.BlockSpec((8, D), lambda i: (jnp.maximum(i * (tm // 8) - 1, 0), 0)),
            pl.BlockSpec((8, D), lambda i: (jnp.minimum((i + 1) * (tm // 8), nb8 - 1), 0)),
        ]
        xs = (x, x, x)
    row = lambda i: (i, 0)
    widths = (SSD_DI, SSD_DI, SSD_G * SSD_N, SSD_G * SSD_N, LANES)
    dts = (BF16, BF16, BF16, BF16, F32)
    return pl.pallas_call(
        functools.partial(_ssd_in_kernel, colmajor=colmajor, blocks_per_seq=part.t // tm, tm=tm),
        grid=(m // tm,),
        in_specs=x_specs + [
            part.mod_spec(tm),
            pl.BlockSpec((1, D), const),
            pl.BlockSpec((D, SSD_DI), const),
            pl.BlockSpec((D, SSD_XBC), const),
            pl.BlockSpec((D, LANES), const),
            pl.BlockSpec((SSD_CW, SSD_XBC), const),
            pl.BlockSpec((1, SSD_XBC), const),
        ],
        out_specs=[pl.BlockSpec((tm, w), row) for w in widths],
        out_shape=[jax.ShapeDtypeStruct((m, w), dt) for w, dt in zip(widths, dts)],
        compiler_params=_cp(1),
        name="ssd_in",
    )(*xs, mod, ng, wz, wx, wdt, cw, cb)


def _ssd_scan_kernel(*refs, nchunks, has_s0):
    if has_s0:
        (z_ref, x_ref, b_ref, c_ref, dt_ref, dtb_ref, alog_ref, dsk_ref, ng_ref, s0_ref,
         y_ref, h_scr, yb_scr) = refs
        sfin_ref = None
    else:
        (z_ref, x_ref, b_ref, c_ref, dt_ref, dtb_ref, alog_ref, dsk_ref, ng_ref,
         y_ref, sfin_ref, h_scr, yb_scr) = refs
    c = CHUNK
    grp = pl.program_id(1)
    shift = (LANES - 2 * SSD_HPG * grp) % LANES
    row = lax.broadcasted_iota(jnp.int32, (c, c), 0)
    col = lax.broadcasted_iota(jnp.int32, (c, c), 1)
    a_row = -jnp.exp(alog_ref[...])
    dtb = dtb_ref[...]
    for d in (1, 0):
        keep = (row >= col) if d == 0 else (row <= col)
        tri = jnp.where(keep, 1.0, 0.0).astype(BF16)
        for hh in range(SSD_HPG):
            if has_s0:
                h_scr[hh] = s0_ref[d, hh]
            else:
                h_scr[hh] = jnp.zeros((SSD_N, SSD_P), F32)

        def body(i, carry, d=d, keep=keep, tri=tri):
            ci = i if d == 0 else nchunks - 1 - i
            sl = pl.ds(pl.multiple_of(ci * c, c), c)
            dt = _softplus(pltpu.roll(dt_ref[sl, :], shift, 1) + dtb)
            dta_hi, dta_lo = _hi_lo(dt * a_row)
            cum = _dot(tri, dta_hi) + _dot(tri, dta_lo)
            cum_t = cum.T
            dt_t = dt.T
            total = cum[c - 1:c, :] if d == 0 else cum[0:1, :]
            f_in = jnp.exp(total - cum) * dt
            bm = b_ref[sl, :]
            cm = c_ref[sl, :]
            bm32 = bm.astype(F32)
            cm32 = cm.astype(F32)
            xb = x_ref[sl, :]
            cb = _dot_nt(cm, bm)
            ys = []
            for hh in range(SSD_HPG):
                l = d * SSD_HPG + hh
                ccol = jnp.broadcast_to(cum[:, l:l + 1], (c, c))
                w = jnp.exp(jnp.where(keep, ccol - cum_t[l:l + 1, :], -jnp.inf)) * cb * dt_t[l:l + 1, :]
                xh = xb[:, hh * SSD_P:(hh + 1) * SSD_P]
                hst = h_scr[hh]
                lhs = jnp.concatenate([w.astype(BF16), (cm32 * jnp.exp(ccol)).astype(BF16)], axis=1)
                rhs = jnp.concatenate([xh, hst.astype(BF16)], axis=0)
                ys.append(_dot(lhs, rhs))
                bs = (bm32 * jnp.broadcast_to(f_in[:, l:l + 1], (c, c))).astype(BF16)
                h_scr[hh] = hst * jnp.exp(total[:, l:l + 1]) + _dot_tn(bs, xh)
            y = jnp.concatenate(ys, axis=1)
            if d == 1:
                yb_scr[sl, :] = y
            else:
                x32 = xb.astype(F32)
                y = y + yb_scr[sl, :] + dsk_ref[...] * x32
                y = y * _silu(z_ref[sl, :].astype(F32))
                y_ref[sl, :] = _rms(y, ng_ref[...]).astype(y_ref.dtype)
            return carry

        lax.fori_loop(0, nchunks, body, 0)
        if sfin_ref is not None:
            for hh in range(SSD_HPG):
                sfin_ref[d, hh] = h_scr[hh]


def _ssd_scan(part, z, xc, bc, cc, dt, dtb, alog, dsk, ng, s0):
    t = part.t
    has_s0 = s0 is not None
    gw = SSD_DI // SSD_G
    sg = lambda b, g: (b, g)
    grp_row = lambda b, g: (g, 0, 0)
    state_spec = pl.BlockSpec((None, 2, SSD_HPG, SSD_N, SSD_P), lambda b, g: (b, 0, g, 0, 0))
    in_specs = [
        pl.BlockSpec((t, gw), sg),
        pl.BlockSpec((t, gw), sg),
        pl.BlockSpec((t, SSD_N), sg),
        pl.BlockSpec((t, SSD_N), sg),
        pl.BlockSpec((t, LANES), lambda b, g: (b, 0)),
        pl.BlockSpec((None, 1, LANES), grp_row),
        pl.BlockSpec((None, 1, LANES), grp_row),
        pl.BlockSpec((None, 1, gw), grp_row),
        pl.BlockSpec((None, 1, gw), grp_row),
    ]
    args = [z, xc, bc, cc, dt, dtb, alog, dsk, ng]
    y_spec = pl.BlockSpec((t, gw), sg)
    y_shape = jax.ShapeDtypeStruct((part.m, SSD_DI), BF16)
    if has_s0:
        in_specs.append(state_spec)
        args.append(s0)
        out_specs, out_shape = y_spec, y_shape
    else:
        out_specs = [y_spec, state_spec]
        out_shape = [y_shape, jax.ShapeDtypeStruct((part.nseq, 2, SSD_NH, SSD_N, SSD_P), F32)]
    res = pl.pallas_call(
        functools.partial(_ssd_scan_kernel, nchunks=t // CHUNK, has_s0=has_s0),
        grid=(part.nseq, SSD_G),
        in_specs=in_specs,
        out_specs=out_specs,
        out_shape=out_shape,
        scratch_shapes=[pltpu.VMEM((SSD_HPG, SSD_N, SSD_P), F32), pltpu.VMEM((t, gw), F32)],
        compiler_params=_cp(2),
        name="ssd_scan",
    )(*args)
    return (res, None) if has_s0 else res


def _ssd_out_kernel(y_ref, x_ref, mod_ref, ng1_ref, ng2_ref, w_ref, xn_ref, h2_ref, *, colmajor):
    x = x_ref[...]
    if colmajor:
        x = _cols_to_rows(x, SSD_NCOL)
    xn, h2 = _mix_out_tail(_dot(y_ref[...], w_ref[...]), x, mod_ref, ng1_ref, ng2_ref)
    if colmajor:
        xn = _rows_to_cols(xn, SSD_NCOL)
        h2 = _rows_to_cols(h2, SSD_NCOL)
    xn_ref[...] = xn
    h2_ref[...] = h2


def _ssd_out(part, y, x, mod, ng1, ng2, w, colmajor):
    m = part.m
    const = lambda i: (0, 0)
    row = lambda i: (i, 0)
    if colmajor:
        tm = SSD_NCOL * GRID_W
        per_b = GRID_W // SSD_NCOL
        vshape = (part.nseq * GRID_W, GRID_W * D)
        xspec = pl.BlockSpec((GRID_W, SSD_NCOL * D), lambda i: (i // per_b, i % per_b))
        x = x.reshape(vshape)
    else:
        tm = 512
        vshape = (m, D)
        xspec = pl.BlockSpec((tm, D), row)
    xn, h2 = pl.pallas_call(
        functools.partial(_ssd_out_kernel, colmajor=colmajor),
        grid=(m // tm,),
        in_specs=[
            pl.BlockSpec((tm, SSD_DI), row),
            xspec,
            part.mod_spec(tm),
            pl.BlockSpec((1, D), const),
            pl.BlockSpec((1, D), const),
            pl.BlockSpec((SSD_DI, D), const),
        ],
        out_specs=[xspec, xspec],
        out_shape=[jax.ShapeDtypeStruct(vshape, F32), jax.ShapeDtypeStruct(vshape, F32)],
        compiler_params=_cp(1),
        name="ssd_out",
    )(y, x, mod, ng1, ng2, w)
    return xn.reshape(m, D), h2.reshape(m, D)


ROUTER_TM = 512
R_E0, R_E1, R_G0, R_G1, R_RANK0, R_RANK1 = range(6)


def _router_kernel(h_ref, w_ref, r_ref, cnt_ref, carry_scr):
    tm = ROUTER_TM

    @pl.when(pl.program_id(0) == 0)
    def _():
        carry_scr[...] = jnp.zeros_like(carry_scr)

    lane = lax.broadcasted_iota(jnp.int32, (tm, LANES), 1).astype(F32)
    logits = jnp.where(lane < N_EXP, _dot(h_ref[...].astype(BF16), w_ref[...]), -jnp.inf)
    m0 = jnp.max(logits, axis=1, keepdims=True)
    i0 = jnp.min(jnp.where(logits == m0, lane, float(LANES)), axis=1, keepdims=True)
    pick0 = lane == i0
    rest = jnp.where(pick0, -jnp.inf, logits)
    m1 = jnp.max(rest, axis=1, keepdims=True)
    i1 = jnp.min(jnp.where(rest == m1, lane, float(LANES)), axis=1, keepdims=True)
    pick1 = lane == i1
    e1 = jnp.exp(m1 - m0)
    den = 1.0 + e1
    oh0 = jnp.where(pick0, 1.0, 0.0)
    oh1 = jnp.where(pick1, 1.0, 0.0)
    oh = oh0 + oh1
    r_i = lax.broadcasted_iota(jnp.int32, (tm, tm), 0)
    c_i = lax.broadcasted_iota(jnp.int32, (tm, tm), 1)
    before = jnp.where(r_i > c_i, 1.0, 0.0).astype(BF16)
    base = _dot(before, oh.astype(BF16)) + carry_scr[...]
    rank0 = jnp.sum(base * oh0, axis=1, keepdims=True)
    rank1 = jnp.sum(base * oh1, axis=1, keepdims=True)
    carry_scr[...] += jnp.sum(oh, axis=0, keepdims=True)
    out = jnp.zeros((tm, LANES), F32)
    for idx, val in ((R_E0, i0), (R_E1, i1), (R_G0, 1.0 / den), (R_G1, e1 / den),
                     (R_RANK0, rank0), (R_RANK1, rank1)):
        out = jnp.where(lane == idx, val, out)
    r_ref[...] = out
    cnt_ref[...] = jnp.broadcast_to(carry_scr[...], (8, LANES))


def _router(part, h2, w):
    tm = ROUTER_TM
    return pl.pallas_call(
        _router_kernel,
        grid=(part.m // tm,),
        in_specs=[pl.BlockSpec((tm, D), lambda i: (i, 0)), pl.BlockSpec((D, LANES), lambda i: (0, 0))],
        out_specs=[pl.BlockSpec((tm, LANES), lambda i: (i, 0)), pl.BlockSpec((8, LANES), lambda i: (0, 0))],
        out_shape=[jax.ShapeDtypeStruct((part.m, LANES), F32), jax.ShapeDtypeStruct((8, LANES), F32)],
        scratch_shapes=[pltpu.VMEM((1, LANES), F32)],
        compiler_params=_cp(1),
        name="router",
    )(h2, w)


MOE_TM = 256


def _row_copy(src_ref, src_row, dst_ref, dst_row, sem):
    return pltpu.make_async_copy(src_ref.at[pl.ds(src_row, 1), :], dst_ref.at[pl.ds(dst_row, 1), :], sem)


def _dispatch_kernel(d0_ref, d1_ref, h_ref, slots_in_ref, slots_ref, sem):
    del slots_in_ref
    tm = MOE_TM

    def start(r, carry):
        _row_copy(h_ref, r, slots_ref, d0_ref[0, 0, r], sem).start()
        _row_copy(h_ref, r, slots_ref, d1_ref[0, 0, r], sem).start()
        return carry

    def wait(r, carry):
        _row_copy(h_ref, r, slots_ref, d0_ref[0, 0, r], sem).wait()
        _row_copy(h_ref, r, slots_ref, d1_ref[0, 0, r], sem).wait()
        return carry

    lax.fori_loop(0, tm, start, 0)
    lax.fori_loop(0, tm, wait, 0)


def _dispatch(part, h2, dest0, dest1, n_slots):
    tm = MOE_TM
    nb = part.m // tm
    idx_spec = pl.BlockSpec((1, 1, tm), lambda i: (i, 0, 0), memory_space=pltpu.SMEM)
    return pl.pallas_call(
        _dispatch_kernel,
        grid=(nb,),
        in_specs=[idx_spec, idx_spec, pl.BlockSpec((tm, D), lambda i: (i, 0)), pl.BlockSpec(memory_space=pl.ANY)],
        out_specs=pl.BlockSpec(memory_space=pl.ANY),
        out_shape=jax.ShapeDtypeStruct((n_slots, D), F32),
        scratch_shapes=[pltpu.SemaphoreType.DMA(())],
        input_output_aliases={3: 0},
        compiler_params=_cp(1),
        name="moe_dispatch",
    )(dest0.reshape(nb, 1, tm), dest1.reshape(nb, 1, tm), h2, jnp.zeros((n_slots, D), F32))


def _expert_kernel(be_ref, nv_ref, xs_ref, wg_ref, wu_ref, wd_ref, ys_ref):
    del be_ref
    j = pl.program_id(0)

    @pl.when(j < nv_ref[0])
    def _():
        x = xs_ref[...].astype(BF16)
        acc = None
        for f in range(EXP_DIM // EXP_FCHUNK):
            fs = slice(f * EXP_FCHUNK, (f + 1) * EXP_FCHUNK)
            hd = (_silu(_dot(x, wg_ref[:, fs])) * _dot(x, wu_ref[:, fs])).astype(BF16)
            part = _dot(hd, wd_ref[fs, :])
            acc = part if acc is None else acc + part
        ys_ref[...] = acc

    @pl.when(j >= nv_ref[0])
    def _():
        ys_ref[...] = jnp.zeros_like(ys_ref)


def _experts(slots, blk_e, n_valid, wg, wu, wd):
    n_blocks = slots.shape[0] // EXP_ROWS
    grid_spec = pltpu.PrefetchScalarGridSpec(
        num_scalar_prefetch=2,
        grid=(n_blocks,),
        in_specs=[
            pl.BlockSpec((EXP_ROWS, D), lambda j, be, nv: (j, 0)),
            pl.BlockSpec((None, D, EXP_DIM), lambda j, be, nv: (be[j], 0, 0)),
            pl.BlockSpec((None, D, EXP_DIM), lambda j, be, nv: (be[j], 0, 0)),
            pl.BlockSpec((None, EXP_DIM, D), lambda j, be, nv: (be[j], 0, 0)),
        ],
        out_specs=pl.BlockSpec((EXP_ROWS, D), lambda j, be, nv: (j, 0)),
    )
    return pl.pallas_call(
        _expert_kernel,
        grid_spec=grid_spec,
        out_shape=jax.ShapeDtypeStruct(slots.shape, F32),
        compiler_params=_cp(1, 60 * 1024 * 1024),
        name="moe_experts",
    )(blk_e, n_valid, slots, wg, wu, wd)


def _combine_kernel(d0_ref, d1_ref, r_ref, x_ref, mod_ref, ng_ref, ys_ref, o_ref, y0_scr, y1_scr, sem):
    tm = MOE_TM

    def start(r, carry):
        _row_copy(ys_ref, d0_ref[0, 0, r], y0_scr, r, sem).start()
        _row_copy(ys_ref, d1_ref[0, 0, r], y1_scr, r, sem).start()
        return carry

    def wait(r, carry):
        _row_copy(ys_ref, d0_ref[0, 0, r], y0_scr, r, sem).wait()
        _row_copy(ys_ref, d1_ref[0, 0, r], y1_scr, r, sem).wait()
        return carry

    lax.fori_loop(0, tm, start, 0)
    lax.fori_loop(0, tm, wait, 0)
    rt = r_ref[...]
    ff = rt[:, R_G0:R_G0 + 1] * y0_scr[...] + rt[:, R_G1:R_G1 + 1] * y1_scr[...]
    o_ref[...] = x_ref[...] + mod_ref[0][5:6] * _rms(ff, ng_ref[...])


def _combine(part, ys, dest0, dest1, routes, x, mod, ng):
    tm = MOE_TM
    nb = part.m // tm
    idx_spec = pl.BlockSpec((1, 1, tm), lambda i: (i, 0, 0), memory_space=pltpu.SMEM)
    row = lambda i: (i, 0)
    return pl.pallas_call(
        _combine_kernel,
        grid=(nb,),
        in_specs=[
            idx_spec, idx_spec,
            pl.BlockSpec((tm, LANES), row),
            pl.BlockSpec((tm, D), row),
            part.mod_spec(tm),
            pl.BlockSpec((1, D), lambda i: (0, 0)),
            pl.BlockSpec(memory_space=pl.ANY),
        ],
        out_specs=pl.BlockSpec((tm, D), row),
        out_shape=jax.ShapeDtypeStruct((part.m, D), F32),
        scratch_shapes=[pltpu.VMEM((tm, D), F32), pltpu.VMEM((tm, D), F32), pltpu.SemaphoreType.DMA(())],
        compiler_params=_cp(1),
        name="moe_combine",
    )(dest0.reshape(nb, 1, tm), dest1.reshape(nb, 1, tm), routes, x, mod, ng, ys)


def _moe(part, h2, x, mod, ng, w_router, wg, wu, wd):
    routes, cnt = _router(part, h2, w_router)
    n_asg = 2 * part.m
    n_blocks = n_asg // EXP_ROWS + N_EXP
    counts = cnt[0, :N_EXP].astype(jnp.int32)
    padded = (counts + EXP_ROWS - 1) // EXP_ROWS * EXP_ROWS
    pad_ends = jnp.cumsum(padded)
    pad_starts = pad_ends - padded
    e0 = routes[:, R_E0].astype(jnp.int32)
    e1 = routes[:, R_E1].astype(jnp.int32)
    dest0 = pad_starts[e0] + routes[:, R_RANK0].astype(jnp.int32)
    dest1 = pad_starts[e1] + routes[:, R_RANK1].astype(jnp.int32)
    n_valid = pad_ends[-1] // EXP_ROWS
    blk = jnp.arange(n_blocks, dtype=jnp.int32)
    blk_e = jnp.sum(blk[:, None] * EXP_ROWS >= pad_ends[None, :], axis=1).astype(jnp.int32)
    blk_e = jnp.minimum(blk_e, N_EXP - 1)
    blk_e = jnp.where(blk < n_valid, blk_e, blk_e[jnp.maximum(n_valid - 1, 0)])
    slots = _dispatch(part, h2, dest0, dest1, n_blocks * EXP_ROWS)
    ys = _experts(slots, blk_e, n_valid.reshape(1).astype(jnp.int32), wg, wu, wd)
    return _combine(part, ys, dest0, dest1, routes, x, mod, ng)


def _pad_cols(w, n):
    return jnp.pad(w, ((0, 0), (0, n - w.shape[1])))


def kernel(x_prompt, x_sample, state_gla, state_ssd, c, c_ctx, w_mod, b_mod, norm_g, gla_w_in, gla_w_a2, gla_b_a,
           gla_norm_g, gla_w_out, ssd_w_in, ssd_conv_w, ssd_conv_b, ssd_dt_bias, ssd_a_log, ssd_d, ssd_norm_g,
           ssd_w_out, ffn_w_gate, ffn_w_up, ffn_w_down, moe_w_router, moe_w_gate, moe_w_up, moe_w_down):
    bp, seq, _ = x_prompt.shape
    bd, dseq, _ = x_sample.shape
    parts = (_Part(bp, seq, 0, False), _Part(bd, dseq, 1, True))
    xs = [x_prompt.reshape(bp * seq, D), x_sample.reshape(bd * dseq, D)]

    c_all = jnp.concatenate([c_ctx[None], c, jnp.zeros((8 - 1 - bd, D), F32)], axis=0)
    mods = _modulation(c_all, w_mod, b_mod).reshape(DEPTH, 8, 6, D)

    gla_states, ssd_states = [], []
    for l in range(DEPTH):
        j = l // 2
        mod = mods[l]
        ng = [norm_g[l, i].reshape(1, D) for i in range(4)]
        if l % 2 == 0:
            w_in = _pad_cols(gla_w_in[j], sum(GLA_SEGS)).astype(BF16)
            wa = jnp.zeros((2, LANES, GLA_DK), F32)
            wa = wa.at[0, :GLA_RANK].set(gla_w_a2[j, 0]).at[1, GLA_RANK:2 * GLA_RANK].set(gla_w_a2[j, 1])
            ba = gla_b_a[j].reshape(2, 1, GLA_DK)
            gng = gla_norm_g[j].reshape(1, GLA_HV)
            w_out = gla_w_out[j].astype(BF16)
            wg, wu, wd = ffn_w_gate[j].astype(BF16), ffn_w_up[j].astype(BF16), ffn_w_down[j].astype(BF16)
            for pi, part in enumerate(parts):
                q, k, v, r, a = _gla_in(part, xs[pi], mod, ng[0], w_in)
                s0 = state_gla[:, j] if pi == 1 else None
                o, sfin = _gla_scan(part, q, k, v, a, wa, ba, s0)
                if pi == 0:
                    gla_states.append(sfin)
                xn, h2 = _gla_out(part, o, r, xs[pi], mod, gng, ng[1], ng[2], w_out)
                xs[pi] = _ffn(part, h2, xn, mod, ng[3], wg, wu, wd)
        else:
            w_in = ssd_w_in[j]
            wz = w_in[:, :SSD_DI].astype(BF16)
            wx = w_in[:, SSD_DI:SSD_DI + SSD_XBC].astype(BF16)
            wdt = w_in[:, SSD_DI + SSD_XBC:].reshape(D, 2, SSD_G, SSD_HPG).transpose(0, 2, 1, 3)
            wdt = _pad_cols(wdt.reshape(D, 2 * SSD_NH), LANES).astype(BF16)
            per_grp = lambda p: _pad_cols(p.reshape(2, SSD_G, SSD_HPG).transpose(1, 0, 2).reshape(SSD_G, 2 * SSD_HPG),
                                          LANES).reshape(SSD_G, 1, LANES)
            dtb = per_grp(ssd_dt_bias[j])
            alog = per_grp(ssd_a_log[j])
            gw = SSD_DI // SSD_G
            dsk = jnp.repeat(ssd_d[j], SSD_P).reshape(SSD_G, 1, gw)
            sng = ssd_norm_g[j].reshape(SSD_G, 1, gw)
            cw = ssd_conv_w[j]
            cb = ssd_conv_b[j].reshape(1, SSD_XBC)
            w_out = ssd_w_out[j].astype(BF16)
            w_router = _pad_cols(moe_w_router[j], LANES).astype(BF16)
            wg, wu, wd = moe_w_gate[j].astype(BF16), moe_w_up[j].astype(BF16), moe_w_down[j].astype(BF16)
            for pi, part in enumerate(parts):
                colmajor = pi == 1
                z, xc, bc, cc, dt = _ssd_in(part, xs[pi], mod, ng[0], wz, wx, wdt, cw, cb, colmajor)
                s0 = state_ssd[:, j] if pi == 1 else None
                y, sfin = _ssd_scan(part, z, xc, bc, cc, dt, dtb, alog, dsk, sng, s0)
                if pi == 0:
                    ssd_states.append(sfin)
                xn, h2 = _ssd_out(part, y, xs[pi], mod, ng[1], ng[2], w_out, colmajor)
                xs[pi] = _moe(part, h2, xn, mod, ng[3], w_router, wg, wu, wd)

    y_prompt = xs[0].reshape(bp, seq, D)
    y_sample = xs[1].reshape(bd, dseq, D)
    return y_prompt, y_sample, jnp.stack(gla_states, axis=1), jnp.stack(ssd_states, axis=1)
```

```python
import functools

import jax
import jax.numpy as jnp
from jax import lax
from jax.experimental import pallas as pl
from jax.experimental.pallas import tpu as pltpu

F32 = jnp.float32
BF16 = jnp.bfloat16

D = 1024
DEPTH = 4
EPS = 1e-6
GRID_W = 64
GLA_H = 4
GLA_HK = 128
GLA_HV = 256
GLA_DK = 512
GLA_DV = 1024
GLA_RANK = 16
SSD_DI = 2048
SSD_P = 64
SSD_NH = 32
SSD_G = 4
SSD_HPG = 8
SSD_N = 128
SSD_CW = 5
SSD_XBC = SSD_DI + 2 * SSD_G * SSD_N
SSD_GW = SSD_DI // SSD_G
FFN_DIM = 2816
N_EXP = 8
EXP_DIM = 3584
CHUNK = 128
LANES = 128
EXP_ROWS = 256
EXP_FCHUNK = 512
VMEM_LIMIT = 56 * 1024 * 1024
LOG2E = 1.4426950408889634


def _cp(n_axes=1, vmem=VMEM_LIMIT):
    return pltpu.CompilerParams(dimension_semantics=("arbitrary",) * n_axes, vmem_limit_bytes=vmem)


def _dot(a, b):
    return jnp.dot(a, b, preferred_element_type=F32)


def _dot_nt(a, b):
    return lax.dot_general(a, b, (((1,), (1,)), ((), ())), preferred_element_type=F32)


def _dot_tn(a, b):
    return lax.dot_general(a, b, (((0,), (0,)), ((), ())), preferred_element_type=F32)


def _rms(x, g):
    ms = jnp.mean(x * x, axis=-1, keepdims=True)
    return x * lax.rsqrt(ms + EPS) * g


def _silu(x):
    return x * jax.nn.sigmoid(x)


def _softplus(x):
    return jnp.maximum(x, 0.0) + jnp.log1p(jnp.exp(-jnp.abs(x)))


def _hi_lo(x):
    hi = x.astype(BF16)
    lo = (x - hi.astype(F32)).astype(BF16)
    return hi, lo


def _tri_masks(c):
    row = lax.broadcasted_iota(jnp.int32, (c, c), 0)
    col = lax.broadcasted_iota(jnp.int32, (c, c), 1)
    keeps = (row >= col, row <= col)
    tris = tuple(jnp.where(k, 1.0, 0.0).astype(BF16) for k in keeps)
    return keeps, tris


def _mod_kernel(c_ref, w_ref, b_ref, o_ref):
    s = _silu(c_ref[...]).astype(BF16)
    o_ref[0] = _dot(s, w_ref[0].astype(BF16)) + b_ref[0]


def _modulation(c_all, w_mod, b_mod):
    tn = 1536
    return pl.pallas_call(
        _mod_kernel,
        grid=(DEPTH, 6 * D // tn),
        in_specs=[
            pl.BlockSpec((8, D), lambda l, n: (0, 0)),
            pl.BlockSpec((1, D, tn), lambda l, n: (l, 0, n)),
            pl.BlockSpec((1, 1, tn), lambda l, n: (l, 0, n)),
        ],
        out_specs=pl.BlockSpec((1, 8, tn), lambda l, n: (l, 0, n)),
        out_shape=jax.ShapeDtypeStruct((DEPTH, 8, 6 * D), F32),
        compiler_params=_cp(2),
        name="modulation",
    )(c_all, w_mod, b_mod.reshape(DEPTH, 1, 6 * D))


class _Part:
    def __init__(self, nseq, t, mod_base, mod_per_seq):
        self.nseq, self.t, self.m = nseq, t, nseq * t
        self.mod_base = mod_base
        self.rows_per_mod = t if mod_per_seq else nseq * t

    def mod_spec(self, tm):
        base, per = self.mod_base, self.rows_per_mod
        return pl.BlockSpec((1, 6, D), lambda i: (base + (i * tm) // per, 0, 0))


def _modulate(x, mod_ref, ng_ref, shift, scale):
    mod = mod_ref[0]
    return _rms(x, ng_ref[...]) * (1.0 + mod[scale:scale + 1]) + mod[shift:shift + 1]


GLA_SEGS = (GLA_DK, GLA_DK, GLA_DV, GLA_DV, LANES)


def _gla_in_kernel(x_ref, mod_ref, ng_ref, w_ref, q_ref, k_ref, v_ref, r_ref, a_ref):
    h = _modulate(x_ref[...], mod_ref, ng_ref, 0, 1).astype(BF16)
    off = 0
    for o_ref, n in zip((q_ref, k_ref, v_ref, r_ref, a_ref), GLA_SEGS):
        o_ref[...] = _dot(h, w_ref[:, off:off + n]).astype(o_ref.dtype)
        off += n


def _gla_in(part, x, mod, ng, w):
    tm = 512
    row = lambda i: (i, 0)
    const = lambda i: (0, 0)
    return pl.pallas_call(
        _gla_in_kernel,
        grid=(part.m // tm,),
        in_specs=[
            pl.BlockSpec((tm, D), row),
            part.mod_spec(tm),
            pl.BlockSpec((1, D), const),
            pl.BlockSpec((D, sum(GLA_SEGS)), const),
        ],
        out_specs=[pl.BlockSpec((tm, n), row) for n in GLA_SEGS],
        out_shape=[jax.ShapeDtypeStruct((part.m, n), BF16) for n in GLA_SEGS],
        compiler_params=_cp(1),
        name="gla_in",
    )(x, mod, ng, w)


def _gla_scan_kernel(*refs, nchunks, hp, has_s0):
    if has_s0:
        q_ref, k_ref, v_ref, a_ref, wa_ref, ba_ref, s0_ref, o_ref, st_scr = refs
        sfin_ref = None
    else:
        q_ref, k_ref, v_ref, a_ref, wa_ref, ba_ref, o_ref, sfin_ref, st_scr = refs
    c = CHUNK
    keeps, tris = _tri_masks(c)
    scale = GLA_HK ** -0.5
    was = [wa_ref[d].astype(BF16) for d in (0, 1)]
    bas = [ba_ref[d] for d in (0, 1)]
    for d in (0, 1):
        for h in range(hp):
            st_scr[d, h] = s0_ref[d, h].T if has_s0 else jnp.zeros((GLA_HV, GLA_HK), F32)

    def chunk(d, ci, first):
        sl = pl.ds(pl.multiple_of(ci * c, c), c)
        a = _dot(a_ref[sl, :], was[d]) + bas[d]
        g = (jnp.minimum(a, 0.0) - jnp.log1p(jnp.exp(-jnp.abs(a)))) * (1.0 / 16.0)
        g_hi, g_lo = _hi_lo(g)
        b = _dot(tris[d], g_hi) + _dot(tris[d], g_lo)
        b_last = b[c - 1:c, :] if d == 0 else b[0:1, :]
        k = k_ref[sl, :].astype(F32)
        qg = (q_ref[sl, :].astype(F32) * scale * jnp.exp(b)).astype(BF16)
        kg = (k * jnp.exp(-b)).astype(BF16)
        kd = (k * jnp.exp(b_last - b)).astype(BF16)
        dec = jnp.exp(b_last)
        for h in range(hp):
            ks = slice(h * GLA_HK, (h + 1) * GLA_HK)
            vs = slice(h * GLA_HV, (h + 1) * GLA_HV)
            v = v_ref[sl, vs]
            att = jnp.where(keeps[d], _dot_nt(qg[:, ks], kg[:, ks]), 0.0).astype(BF16)
            st = st_scr[d, h]
            o = _dot(att, v) + _dot_nt(qg[:, ks], st.astype(BF16))
            if first:
                o_ref[sl, vs] = o
            else:
                o_ref[sl, vs] += o
            st_scr[d, h] = st * dec[:, ks] + _dot_tn(v, kd[:, ks])

    half = nchunks // 2

    def first_half(i, carry):
        chunk(0, i, True)
        chunk(1, nchunks - 1 - i, True)
        return carry

    def second_half(i, carry):
        chunk(0, i, False)
        chunk(1, nchunks - 1 - i, False)
        return carry

    unroll = 2 if half % 2 == 0 else 1
    lax.fori_loop(0, half, first_half, 0, unroll=unroll)
    lax.fori_loop(half, nchunks, second_half, 0, unroll=unroll)
    if sfin_ref is not None:
        for d in (0, 1):
            for h in range(hp):
                sfin_ref[d, h] = st_scr[d, h].T


def _gla_scan(part, q, k, v, a, wa, ba, s0):
    t = part.t
    nchunks = t // CHUNK
    assert nchunks % 2 == 0
    hp = GLA_H if t <= 512 else 2
    has_s0 = s0 is not None
    seq_head = lambda b, h: (b, h)
    state_spec = pl.BlockSpec((None, 2, hp, GLA_HK, GLA_HV), lambda b, h: (b, 0, h, 0, 0))
    in_specs = [
        pl.BlockSpec((t, hp * GLA_HK), seq_head),
        pl.BlockSpec((t, hp * GLA_HK), seq_head),
        pl.BlockSpec((t, hp * GLA_HV), seq_head),
        pl.BlockSpec((t, LANES), lambda b, h: (b, 0)),
        pl.BlockSpec((2, LANES, hp * GLA_HK), lambda b, h: (0, 0, h)),
        pl.BlockSpec((2, 1, hp * GLA_HK), lambda b, h: (0, 0, h)),
    ]
    args = [q, k, v, a, wa, ba]
    o_spec = pl.BlockSpec((t, hp * GLA_HV), seq_head)
    o_shape = jax.ShapeDtypeStruct((part.m, GLA_DV), F32)
    if has_s0:
        in_specs.append(state_spec)
        args.append(s0)
        out_specs, out_shape = o_spec, o_shape
    else:
        out_specs = [o_spec, state_spec]
        out_shape = [o_shape, jax.ShapeDtypeStruct((part.nseq, 2, GLA_H, GLA_HK, GLA_HV), F32)]
    res = pl.pallas_call(
        functools.partial(_gla_scan_kernel, nchunks=nchunks, hp=hp, has_s0=has_s0),
        grid=(part.nseq, GLA_H // hp),
        in_specs=in_specs,
        out_specs=out_specs,
        out_shape=out_shape,
        scratch_shapes=[pltpu.VMEM((2, hp, GLA_HV, GLA_HK), F32)],
        compiler_params=_cp(2),
        name="gla_scan",
    )(*args)
    return (res, None) if has_s0 else res


def _mix_out_tail(mix, x, mod_ref, ng1_ref, ng2_ref):
    mod = mod_ref[0]
    xn = x + mod[2:3] * _rms(mix, ng1_ref[...])
    h2 = _rms(xn, ng2_ref[...]) * (1.0 + mod[4:5]) + mod[3:4]
    return xn, h2


def _gla_out_kernel(o_ref, r_ref, x_ref, mod_ref, gng_ref, ng1_ref, ng2_ref, w_ref, xn_ref, h2_ref):
    o = o_ref[...]
    gng = gng_ref[...]
    on = jnp.concatenate([_rms(o[:, h * GLA_HV:(h + 1) * GLA_HV], gng) for h in range(GLA_H)], axis=1)
    on = (on * _silu(r_ref[...].astype(F32))).astype(BF16)
    xn, h2 = _mix_out_tail(_dot(on, w_ref[...]), x_ref[...], mod_ref, ng1_ref, ng2_ref)
    xn_ref[...] = xn
    h2_ref[...] = h2.astype(h2_ref.dtype)


def _gla_out(part, o, r, x, mod, gng, ng1, ng2, w):
    tm = 512
    row = lambda i: (i, 0)
    const = lambda i: (0, 0)
    return pl.pallas_call(
        _gla_out_kernel,
        grid=(part.m // tm,),
        in_specs=[
            pl.BlockSpec((tm, GLA_DV), row),
            pl.BlockSpec((tm, GLA_DV), row),
            pl.BlockSpec((tm, D), row),
            part.mod_spec(tm),
            pl.BlockSpec((1, GLA_HV), const),
            pl.BlockSpec((1, D), const),
            pl.BlockSpec((1, D), const),
            pl.BlockSpec((GLA_DV, D), const),
        ],
        out_specs=[pl.BlockSpec((tm, D), row), pl.BlockSpec((tm, D), row)],
        out_shape=[jax.ShapeDtypeStruct((part.m, D), F32), jax.ShapeDtypeStruct((part.m, D), BF16)],
        compiler_params=_cp(1),
        name="gla_out",
    )(o, r, x, mod, gng, ng1, ng2, w)


def _gla_layer_mixer(part, x, mod, ng, w_in, w_a2, b_a, gng, w_out, s0):
    w_in = _pad_cols(w_in, sum(GLA_SEGS)).astype(BF16)
    wa = jnp.zeros((2, LANES, GLA_DK), F32)
    wa = wa.at[0, :GLA_RANK].set(w_a2[0]).at[1, GLA_RANK:2 * GLA_RANK].set(w_a2[1])
    q, k, v, r, a = _gla_in(part, x, mod, ng[0], w_in)
    o, sfin = _gla_scan(part, q, k, v, a, wa, b_a.reshape(2, 1, GLA_DK), s0)
    xn, h2 = _gla_out(part, o, r, x, mod, gng.reshape(1, GLA_HV), ng[1], ng[2], w_out.astype(BF16))
    return xn, h2, sfin


def _ffn_kernel(h_ref, x_ref, mod_ref, ng_ref, wg_ref, wu_ref, wd_ref, o_ref):
    h = h_ref[...]
    hd = (_silu(_dot(h, wg_ref[...])) * _dot(h, wu_ref[...])).astype(BF16)
    ff = _dot(hd, wd_ref[...])
    o_ref[...] = x_ref[...] + mod_ref[0][5:6] * _rms(ff, ng_ref[...])


def _ffn(part, h2, x, mod, ng, wg, wu, wd):
    tm = 256
    row = lambda i: (i, 0)
    const = lambda i: (0, 0)
    once = pl.Buffered(1)
    return pl.pallas_call(
        _ffn_kernel,
        grid=(part.m // tm,),
        in_specs=[
            pl.BlockSpec((tm, D), row),
            pl.BlockSpec((tm, D), row),
            part.mod_spec(tm),
            pl.BlockSpec((1, D), const),
            pl.BlockSpec((D, FFN_DIM), const, pipeline_mode=once),
            pl.BlockSpec((D, FFN_DIM), const, pipeline_mode=once),
            pl.BlockSpec((FFN_DIM, D), const, pipeline_mode=once),
        ],
        out_specs=pl.BlockSpec((tm, D), row),
        out_shape=jax.ShapeDtypeStruct((part.m, D), F32),
        compiler_params=_cp(1),
        name="ffn",
    )(h2, x, mod, ng, wg, wu, wd)


SSD_NCOL = 8


def _grid_cols_spec(rows=GRID_W):
    per_b = GRID_W // SSD_NCOL
    return pl.BlockSpec((None, rows, SSD_NCOL, D), lambda i: (i // per_b, 0, i % per_b, 0))


def _read_colmajor(x_ref):
    return jnp.concatenate([x_ref[:, c, :] for c in range(SSD_NCOL)], axis=0)


def _write_colmajor(o_ref, val):
    for c in range(SSD_NCOL):
        o_ref[:, c, :] = val[c * GRID_W:(c + 1) * GRID_W, :].astype(o_ref.dtype)


def _ssd_in_kernel(x_ref, xp_ref, xn_ref, mod_ref, ng_ref, wz_ref, wx_ref, wdt_ref, cw_ref, cb_ref,
                   z_ref, xc_ref, bc_ref, cc_ref, dt_ref, *, colmajor, blocks_per_seq, tm):
    i = pl.program_id(0)
    first = (i % blocks_per_seq) == 0
    last = (i % blocks_per_seq) == blocks_per_seq - 1
    if colmajor:
        x = _read_colmajor(x_ref)
        xp = xp_ref[:, SSD_NCOL - 1, :]
        xn = xn_ref[:, 0, :]
    else:
        x, xp, xn = x_ref[...], xp_ref[...], xn_ref[...]
    xcat = jnp.concatenate([xp, x, xn], axis=0)
    h = _modulate(xcat, mod_ref, ng_ref, 0, 1).astype(BF16)
    hm = h[8:8 + tm]
    z_ref[...] = _dot(hm, wz_ref[...]).astype(z_ref.dtype)
    dt_ref[...] = _dot(hm, wdt_ref[...])
    rows = lax.broadcasted_iota(jnp.int32, (tm + 16, 1), 0)
    inside = jnp.logical_and(rows >= jnp.where(first, 8, 0), rows < jnp.where(last, tm + 8, tm + 16))
    cw = cw_ref[...]
    cb = cb_ref[...]
    wcol = 512
    for cc in range(SSD_XBC // wcol):
        cs = slice(cc * wcol, (cc + 1) * wcol)
        p = jnp.where(inside, _dot(h, wx_ref[:, cs]), 0.0)
        acc = cw[0:1, cs] * p[6:6 + tm]
        for j in range(1, SSD_CW):
            acc = acc + cw[j:j + 1, cs] * p[6 + j:6 + j + tm]
        y = _silu(acc + cb[:, cs])
        if cc < 4:
            xc_ref[:, cs] = y.astype(xc_ref.dtype)
        elif cc == 4:
            bc_ref[...] = y.astype(bc_ref.dtype)
        else:
            cc_ref[...] = y.astype(cc_ref.dtype)


def _ssd_in(part, x, mod, ng, wz, wx, wdt, cw, cb, colmajor):
    m = part.m
    const = lambda i: (0, 0)
    if colmajor:
        tm = SSD_NCOL * GRID_W
        per_b = GRID_W // SSD_NCOL
        xv = x.reshape(part.nseq, GRID_W, GRID_W, D)
        x_specs = [
            _grid_cols_spec(),
            pl.BlockSpec((None, 8, SSD_NCOL, D), lambda i: (i // per_b, GRID_W // 8 - 1,
                                                            jnp.maximum(i % per_b - 1, 0), 0)),
            pl.BlockSpec((None, 8, SSD_NCOL, D), lambda i: (i // per_b, 0, jnp.minimum(i % per_b + 1, per_b - 1), 0)),
        ]
        xs = (xv, xv, xv)
    else:
        tm = part.t
        nb8 = m // 8
        x_specs = [
            pl.BlockSpec((tm, D), lambda i: (i, 0)),
            pl.BlockSpec((8, D), lambda i: (jnp.maximum(i * (tm // 8) - 1, 0), 0)),
            pl.BlockSpec((8, D), lambda i: (jnp.minimum((i + 1) * (tm // 8), nb8 - 1), 0)),
        ]
        xs = (x, x, x)
    row = lambda i: (i, 0)
    widths = (SSD_DI, SSD_DI, SSD_G * SSD_N, SSD_G * SSD_N, LANES)
    dts = (BF16, BF16, BF16, BF16, F32)
    return pl.pallas_call(
        functools.partial(_ssd_in_kernel, colmajor=colmajor, blocks_per_seq=part.t // tm, tm=tm),
        grid=(m // tm,),
        in_specs=x_specs + [
            part.mod_spec(tm),
            pl.BlockSpec((1, D), const),
            pl.BlockSpec((D, SSD_DI), const),
            pl.BlockSpec((D, SSD_XBC), const),
            pl.BlockSpec((D, LANES), const),
            pl.BlockSpec((SSD_CW, SSD_XBC), const),
            pl.BlockSpec((1, SSD_XBC), const),
        ],
        out_specs=[pl.BlockSpec((tm, w), row) for w in widths],
        out_shape=[jax.ShapeDtypeStruct((m, w), dt) for w, dt in zip(widths, dts)],
        compiler_params=_cp(1),
        name="ssd_in",
    )(*xs, mod, ng, wz, wx, wdt, cw, cb)


SSD_PAIRS = SSD_HPG // 2


def _ssd_scan_kernel(*refs, nchunks, has_s0):
    if has_s0:
        (z_ref, x_ref, b_ref, c_ref, dt_ref, dtb_ref, alog_ref, dsk_ref, ng_ref, s0_ref,
         y_ref, h_scr, yacc_scr) = refs
        sfin_ref = None
    else:
        (z_ref, x_ref, b_ref, c_ref, dt_ref, dtb_ref, alog_ref, dsk_ref, ng_ref,
         y_ref, sfin_ref, h_scr, yacc_scr) = refs
    c = CHUNK
    grp = pl.program_id(1)
    shift = (LANES - 2 * SSD_HPG * grp) % LANES
    keeps, tris = _tri_masks(c)
    lane = lax.broadcasted_iota(jnp.int32, (1, LANES), 1)
    low = lane < SSD_P
    m_lo = jnp.where(low, 1.0, 0.0).astype(BF16)
    m_hi = jnp.where(low, 0.0, 1.0).astype(BF16)
    a_row = -jnp.exp(alog_ref[...])
    dtb = dtb_ref[...]
    for d in (0, 1):
        for pp in range(SSD_PAIRS):
            if has_s0:
                h_scr[d, pp] = jnp.concatenate([s0_ref[d, 2 * pp], s0_ref[d, 2 * pp + 1]], axis=1)
            else:
                h_scr[d, pp] = jnp.zeros((SSD_N, 2 * SSD_P), F32)

    def chunk(d, ci, final):
        sl = pl.ds(pl.multiple_of(ci * c, c), c)
        dt = _softplus(pltpu.roll(dt_ref[sl, :], shift, 1) + dtb)
        dta_hi, dta_lo = _hi_lo(dt * a_row)
        cum = (_dot(tris[d], dta_hi) + _dot(tris[d], dta_lo)) * LOG2E
        log_dt = jnp.log(dt) * LOG2E
        total = cum[c - 1:c, :] if d == 0 else cum[0:1, :]
        f_in = jnp.exp2(total - cum + log_dt)
        e_tot = jnp.exp2(total)
        src_t = (cum - log_dt).T
        f_t = f_in.T
        bm = b_ref[sl, :]
        cm = c_ref[sl, :]
        bm_t32 = bm.astype(F32).T
        cm32 = cm.astype(F32)
        xb = x_ref[sl, :]
        cb = _dot_nt(cm, bm)
        ys = []
        for pp in range(SSD_PAIRS):
            xp = xb[:, pp * LANES:(pp + 1) * LANES]
            x2 = jnp.concatenate([xp * m_lo, xp * m_hi], axis=0)
            hst = h_scr[d, pp]
            hb = hst.astype(BF16)
            ws, cs, bs = [], [], []
            for s in (0, 1):
                l = d * SSD_HPG + 2 * pp + s
                ccol = jnp.broadcast_to(cum[:, l:l + 1], (c, c))
                w = jnp.exp2(jnp.where(keeps[d], ccol - src_t[l:l + 1, :], -jnp.inf)) * cb
                ws.append(w.astype(BF16))
                cs.append((cm32 * jnp.exp2(ccol)).astype(BF16))
                bs.append((bm_t32 * f_t[l:l + 1, :]).astype(BF16))
            lhs = jnp.concatenate(ws + cs, axis=1)
            rhs = jnp.concatenate([x2, hb * m_lo, hb * m_hi], axis=0)
            ys.append(_dot(lhs, rhs))
            l0 = d * SSD_HPG + 2 * pp
            dec = jnp.where(low, e_tot[:, l0:l0 + 1], e_tot[:, l0 + 1:l0 + 2])
            h_scr[d, pp] = hst * dec + _dot(jnp.concatenate(bs, axis=1), x2)
        y = jnp.concatenate(ys, axis=1)
        if final:
            y = y + yacc_scr[sl, :] + dsk_ref[...] * xb.astype(F32)
            y = y * _silu(z_ref[sl, :].astype(F32))
            y_ref[sl, :] = _rms(y, ng_ref[...]).astype(y_ref.dtype)
        else:
            yacc_scr[sl, :] = y

    half = nchunks // 2

    def first_half(i, carry):
        chunk(0, i, False)
        chunk(1, nchunks - 1 - i, False)
        return carry

    def second_half(i, carry):
        chunk(0, i, True)
        chunk(1, nchunks - 1 - i, True)
        return carry

    unroll = 2 if half % 2 == 0 else 1
    lax.fori_loop(0, half, first_half, 0, unroll=unroll)
    lax.fori_loop(half, nchunks, second_half, 0, unroll=unroll)
    if sfin_ref is not None:
        for d in (0, 1):
            for pp in range(SSD_PAIRS):
                hst = h_scr[d, pp]
                sfin_ref[d, 2 * pp] = hst[:, :SSD_P]
                sfin_ref[d, 2 * pp + 1] = hst[:, SSD_P:]


def _ssd_scan(part, z, xc, bc, cc, dt, dtb, alog, dsk, ng, s0):
    t = part.t
    nchunks = t // CHUNK
    assert nchunks % 2 == 0
    has_s0 = s0 is not None
    sg = lambda b, g: (b, g)
    grp_row = lambda b, g: (g, 0, 0)
    state_spec = pl.BlockSpec((None, 2, SSD_HPG, SSD_N, SSD_P), lambda b, g: (b, 0, g, 0, 0))
    in_specs = [
        pl.BlockSpec((t, SSD_GW), sg),
        pl.BlockSpec((t, SSD_GW), sg),
        pl.BlockSpec((t, SSD_N), sg),
        pl.BlockSpec((t, SSD_N), sg),
        pl.BlockSpec((t, LANES), lambda b, g: (b, 0)),
        pl.BlockSpec((None, 1, LANES), grp_row),
        pl.BlockSpec((None, 1, LANES), grp_row),
        pl.BlockSpec((None, 1, SSD_GW), grp_row),
        pl.BlockSpec((None, 1, SSD_GW), grp_row),
    ]
    args = [z, xc, bc, cc, dt, dtb, alog, dsk, ng]
    y_spec = pl.BlockSpec((t, SSD_GW), sg)
    y_shape = jax.ShapeDtypeStruct((part.m, SSD_DI), BF16)
    if has_s0:
        in_specs.append(state_spec)
        args.append(s0)
        out_specs, out_shape = y_spec, y_shape
    else:
        out_specs = [y_spec, state_spec]
        out_shape = [y_shape, jax.ShapeDtypeStruct((part.nseq, 2, SSD_NH, SSD_N, SSD_P), F32)]
    res = pl.pallas_call(
        functools.partial(_ssd_scan_kernel, nchunks=nchunks, has_s0=has_s0),
        grid=(part.nseq, SSD_G),
        in_specs=in_specs,
        out_specs=out_specs,
        out_shape=out_shape,
        scratch_shapes=[pltpu.VMEM((2, SSD_PAIRS, SSD_N, 2 * SSD_P), F32), pltpu.VMEM((t, SSD_GW), F32)],
        compiler_params=_cp(2),
        name="ssd_scan",
    )(*args)
    return (res, None) if has_s0 else res


def _ssd_out_kernel(y_ref, x_ref, mod_ref, ng1_ref, ng2_ref, w_ref, xn_ref, h2_ref, *, colmajor):
    x = _read_colmajor(x_ref) if colmajor else x_ref[...]
    xn, h2 = _mix_out_tail(_dot(y_ref[...], w_ref[...]), x, mod_ref, ng1_ref, ng2_ref)
    if colmajor:
        _write_colmajor(xn_ref, xn)
        _write_colmajor(h2_ref, h2)
    else:
        xn_ref[...] = xn
        h2_ref[...] = h2


def _ssd_out(part, y, x, mod, ng1, ng2, w, colmajor):
    m = part.m
    const = lambda i: (0, 0)
    row = lambda i: (i, 0)
    if colmajor:
        tm = SSD_NCOL * GRID_W
        vshape = (part.nseq, GRID_W, GRID_W, D)
        xspec = _grid_cols_spec()
        x = x.reshape(vshape)
    else:
        tm = 512
        vshape = (m, D)
        xspec = pl.BlockSpec((tm, D), row)
    xn, h2 = pl.pallas_call(
        functools.partial(_ssd_out_kernel, colmajor=colmajor),
        grid=(m // tm,),
        in_specs=[
            pl.BlockSpec((tm, SSD_DI), row),
            xspec,
            part.mod_spec(tm),
            pl.BlockSpec((1, D), const),
            pl.BlockSpec((1, D), const),
            pl.BlockSpec((SSD_DI, D), const),
        ],
        out_specs=[xspec, xspec],
        out_shape=[jax.ShapeDtypeStruct(vshape, F32), jax.ShapeDtypeStruct(vshape, F32)],
        compiler_params=_cp(1),
        name="ssd_out",
    )(y, x, mod, ng1, ng2, w)
    return xn.reshape(m, D), h2.reshape(m, D)


def _ssd_layer_mixer(part, x, mod, ng, w_in, conv_w, conv_b, dt_bias, a_log, d_skip, norm_g, w_out, s0, colmajor):
    wz = w_in[:, :SSD_DI].astype(BF16)
    wx = w_in[:, SSD_DI:SSD_DI + SSD_XBC].astype(BF16)
    wdt = w_in[:, SSD_DI + SSD_XBC:].reshape(D, 2, SSD_G, SSD_HPG).transpose(0, 2, 1, 3)
    wdt = _pad_cols(wdt.reshape(D, 2 * SSD_NH), LANES).astype(BF16)
    per_grp = lambda p: _pad_cols(p.reshape(2, SSD_G, SSD_HPG).transpose(1, 0, 2).reshape(SSD_G, 2 * SSD_HPG),
                                  LANES).reshape(SSD_G, 1, LANES)
    dsk = jnp.repeat(d_skip, SSD_P).reshape(SSD_G, 1, SSD_GW)
    sng = norm_g.reshape(SSD_G, 1, SSD_GW)
    z, xc, bc, cc, dt = _ssd_in(part, x, mod, ng[0], wz, wx, wdt, conv_w, conv_b.reshape(1, SSD_XBC), colmajor)
    y, sfin = _ssd_scan(part, z, xc, bc, cc, dt, per_grp(dt_bias), per_grp(a_log), dsk, sng, s0)
    xn, h2 = _ssd_out(part, y, x, mod, ng[1], ng[2], w_out.astype(BF16), colmajor)
    return xn, h2, sfin


ROUTER_TM = 512
R_E0, R_E1, R_G0, R_G1, R_RANK0, R_RANK1 = range(6)


def _router_kernel(h_ref, w_ref, r_ref, cnt_ref, carry_scr):
    tm = ROUTER_TM

    @pl.when(pl.program_id(0) == 0)
    def _():
        carry_scr[...] = jnp.zeros_like(carry_scr)

    lane = lax.broadcasted_iota(jnp.int32, (tm, LANES), 1).astype(F32)
    logits = jnp.where(lane < N_EXP, _dot(h_ref[...].astype(BF16), w_ref[...]), -jnp.inf)
    m0 = jnp.max(logits, axis=1, keepdims=True)
    i0 = jnp.min(jnp.where(logits == m0, lane, float(LANES)), axis=1, keepdims=True)
    pick0 = lane == i0
    rest = jnp.where(pick0, -jnp.inf, logits)
    m1 = jnp.max(rest, axis=1, keepdims=True)
    i1 = jnp.min(jnp.where(rest == m1, lane, float(LANES)), axis=1, keepdims=True)
    pick1 = lane == i1
    e1 = jnp.exp(m1 - m0)
    den = 1.0 + e1
    oh0 = jnp.where(pick0, 1.0, 0.0)
    oh1 = jnp.where(pick1, 1.0, 0.0)
    oh = oh0 + oh1
    r_i = lax.broadcasted_iota(jnp.int32, (tm, tm), 0)
    c_i = lax.broadcasted_iota(jnp.int32, (tm, tm), 1)
    before = jnp.where(r_i > c_i, 1.0, 0.0).astype(BF16)
    base = _dot(before, oh.astype(BF16)) + carry_scr[...]
    rank0 = jnp.sum(base * oh0, axis=1, keepdims=True)
    rank1 = jnp.sum(base * oh1, axis=1, keepdims=True)
    carry_scr[...] += jnp.sum(oh, axis=0, keepdims=True)
    out = jnp.zeros((tm, LANES), F32)
    for idx, val in ((R_E0, i0), (R_E1, i1), (R_G0, 1.0 / den), (R_G1, e1 / den),
                     (R_RANK0, rank0), (R_RANK1, rank1)):
        out = jnp.where(lane == idx, val, out)
    r_ref[...] = out
    cnt_ref[...] = jnp.broadcast_to(carry_scr[...], (8, LANES))


def _router(part, h2, w):
    tm = ROUTER_TM
    return pl.pallas_call(
        _router_kernel,
        grid=(part.m // tm,),
        in_specs=[pl.BlockSpec((tm, D), lambda i: (i, 0)), pl.BlockSpec((D, LANES), lambda i: (0, 0))],
        out_specs=[pl.BlockSpec((tm, LANES), lambda i: (i, 0)), pl.BlockSpec((8, LANES), lambda i: (0, 0))],
        out_shape=[jax.ShapeDtypeStruct((part.m, LANES), F32), jax.ShapeDtypeStruct((8, LANES), F32)],
        scratch_shapes=[pltpu.VMEM((1, LANES), F32)],
        compiler_params=_cp(1),
        name="router",
    )(h2, w)


MOE_TM = 256
DMA_UNROLL = 8


def _row_copy(src_ref, src_row, dst_ref, dst_row, sem):
    return pltpu.make_async_copy(src_ref.at[pl.ds(src_row, 1), :], dst_ref.at[pl.ds(dst_row, 1), :], sem)


def _for_rows(n, fn):
    def body(g, carry):
        for u in range(DMA_UNROLL):
            fn(g * DMA_UNROLL + u, u % 2)
        return carry

    lax.fori_loop(0, n // DMA_UNROLL, body, 0)


def _dispatch_kernel(cnt_ref, pad_ref, start_ref, d0_ref, d1_ref, h_ref, slots_ref, zero_scr, sem, zsem):
    tm = MOE_TM

    def start(r, prio):
        _row_copy(h_ref, r, slots_ref, d0_ref[0, 0, r], sem).start(priority=prio)
        _row_copy(h_ref, r, slots_ref, d1_ref[0, 0, r], sem).start(priority=prio)

    def wait(r, prio):
        _row_copy(h_ref, 0, slots_ref, 0, sem).wait()
        _row_copy(h_ref, 0, slots_ref, 0, sem).wait()

    _for_rows(tm, start)

    @pl.when(pl.program_id(0) == 0)
    def _():
        zero_scr[...] = jnp.zeros_like(zero_scr)
        for e in range(N_EXP + 1):
            first = start_ref[e]

            def fill(r, carry, first=first):
                _row_copy(zero_scr, 0, slots_ref, first + r, zsem).start()
                return carry

            def fill_wait(r, carry):
                _row_copy(zero_scr, 0, slots_ref, 0, zsem).wait()
                return carry

            lax.fori_loop(cnt_ref[e], pad_ref[e], fill, 0)
            lax.fori_loop(cnt_ref[e], pad_ref[e], fill_wait, 0)

    _for_rows(tm, wait)


def _dispatch(part, h2, dest0, dest1, counts, padded, pad_starts, n_slots):
    tm = MOE_TM
    nb = part.m // tm
    idx_spec = pl.BlockSpec((1, 1, tm), lambda i, *_: (i, 0, 0), memory_space=pltpu.SMEM)
    grid_spec = pltpu.PrefetchScalarGridSpec(
        num_scalar_prefetch=3,
        grid=(nb,),
        in_specs=[idx_spec, idx_spec, pl.BlockSpec((tm, D), lambda i, *_: (i, 0))],
        out_specs=pl.BlockSpec(memory_space=pl.ANY),
        scratch_shapes=[pltpu.VMEM((8, D), F32), pltpu.SemaphoreType.DMA(()), pltpu.SemaphoreType.DMA(())],
    )
    return pl.pallas_call(
        _dispatch_kernel,
        grid_spec=grid_spec,
        out_shape=jax.ShapeDtypeStruct((n_slots, D), F32),
        compiler_params=_cp(1),
        name="moe_dispatch",
    )(counts, padded, pad_starts, dest0.reshape(nb, 1, tm), dest1.reshape(nb, 1, tm), h2)


def _expert_kernel(be_ref, nv_ref, xs_ref, wg_ref, wu_ref, wd_ref, ys_ref):
    del be_ref
    used = pl.program_id(0) < nv_ref[0]

    @pl.when(jnp.logical_not(used))
    def _():
        ys_ref[...] = jnp.zeros_like(ys_ref)

    @pl.when(used)
    def _():
        x = xs_ref[...].astype(BF16)
        acc = None
        for f in range(EXP_DIM // EXP_FCHUNK):
            fs = slice(f * EXP_FCHUNK, (f + 1) * EXP_FCHUNK)
            hd = (_silu(_dot(x, wg_ref[:, fs])) * _dot(x, wu_ref[:, fs])).astype(BF16)
            part = _dot(hd, wd_ref[fs, :])
            acc = part if acc is None else acc + part
        ys_ref[...] = acc


def _experts(slots, blk_e, n_valid, wg, wu, wd):
    n_blocks = slots.shape[0] // EXP_ROWS
    rows = lambda j, be, nv: (j, 0)
    grid_spec = pltpu.PrefetchScalarGridSpec(
        num_scalar_prefetch=2,
        grid=(n_blocks,),
        in_specs=[
            pl.BlockSpec((EXP_ROWS, D), rows),
            pl.BlockSpec((None, D, EXP_DIM), lambda j, be, nv: (be[j], 0, 0)),
            pl.BlockSpec((None, D, EXP_DIM), lambda j, be, nv: (be[j], 0, 0)),
            pl.BlockSpec((None, EXP_DIM, D), lambda j, be, nv: (be[j], 0, 0)),
        ],
        out_specs=pl.BlockSpec((EXP_ROWS, D), rows),
    )
    return pl.pallas_call(
        _expert_kernel,
        grid_spec=grid_spec,
        out_shape=jax.ShapeDtypeStruct(slots.shape, F32),
        compiler_params=_cp(1, 60 * 1024 * 1024),
        name="moe_experts",
    )(blk_e, n_valid, slots, wg, wu, wd)


def _combine_kernel(d0_ref, d1_ref, n0_ref, n1_ref, r_ref, x_ref, mod_ref, ng_ref, ys_ref, o_ref,
                    y0_scr, y1_scr, sem):
    tm = MOE_TM
    i = pl.program_id(0)
    nb = pl.num_programs(0)
    slot = i % 2

    def gather(i0_ref, i1_ref, s):
        def start(r, prio):
            _row_copy(ys_ref, i0_ref[0, 0, r], y0_scr.at[s], r, sem.at[s]).start(priority=prio)
            _row_copy(ys_ref, i1_ref[0, 0, r], y1_scr.at[s], r, sem.at[s]).start(priority=prio)

        _for_rows(tm, start)

    @pl.when(i == 0)
    def _():
        gather(d0_ref, d1_ref, 0)

    @pl.when(i + 1 < nb)
    def _():
        gather(n0_ref, n1_ref, 1 - slot)

    def wait(r, prio):
        _row_copy(ys_ref, 0, y0_scr.at[slot], 0, sem.at[slot]).wait()
        _row_copy(ys_ref, 0, y1_scr.at[slot], 0, sem.at[slot]).wait()

    _for_rows(tm, wait)
    rt = r_ref[...]
    ff = rt[:, R_G0:R_G0 + 1] * y0_scr[slot] + rt[:, R_G1:R_G1 + 1] * y1_scr[slot]
    o_ref[...] = x_ref[...] + mod_ref[0][5:6] * _rms(ff, ng_ref[...])


def _combine(part, ys, dest0, dest1, routes, x, mod, ng):
    tm = MOE_TM
    nb = part.m // tm
    idx_spec = pl.BlockSpec((1, 1, tm), lambda i: (i, 0, 0), memory_space=pltpu.SMEM)
    nxt_spec = pl.BlockSpec((1, 1, tm), lambda i: (jnp.minimum(i + 1, nb - 1), 0, 0), memory_space=pltpu.SMEM)
    row = lambda i: (i, 0)
    d0 = dest0.reshape(nb, 1, tm)
    d1 = dest1.reshape(nb, 1, tm)
    return pl.pallas_call(
        _combine_kernel,
        grid=(nb,),
        in_specs=[
            idx_spec, idx_spec, nxt_spec, nxt_spec,
            pl.BlockSpec((tm, LANES), row),
            pl.BlockSpec((tm, D), row),
            part.mod_spec(tm),
            pl.BlockSpec((1, D), lambda i: (0, 0)),
            pl.BlockSpec(memory_space=pl.ANY),
        ],
        out_specs=pl.BlockSpec((tm, D), row),
        out_shape=jax.ShapeDtypeStruct((part.m, D), F32),
        scratch_shapes=[pltpu.VMEM((2, tm, D), F32), pltpu.VMEM((2, tm, D), F32), pltpu.SemaphoreType.DMA((2,))],
        compiler_params=_cp(1),
        name="moe_combine",
    )(d0, d1, d0, d1, routes, x, mod, ng, ys)


def _moe(part, h2, x, mod, ng, w_router, wg, wu, wd):
    routes, cnt = _router(part, h2, w_router)
    n_asg = 2 * part.m
    n_blocks = n_asg // EXP_ROWS + N_EXP
    counts = cnt[0, :N_EXP].astype(jnp.int32)
    padded = (counts + EXP_ROWS - 1) // EXP_ROWS * EXP_ROWS
    pad_ends = jnp.cumsum(padded)
    pad_starts = pad_ends - padded
    e0 = routes[:, R_E0].astype(jnp.int32)
    e1 = routes[:, R_E1].astype(jnp.int32)
    dest0 = pad_starts[e0] + routes[:, R_RANK0].astype(jnp.int32)
    dest1 = pad_starts[e1] + routes[:, R_RANK1].astype(jnp.int32)
    n_valid = pad_ends[-1] // EXP_ROWS
    blk = jnp.arange(n_blocks, dtype=jnp.int32)
    blk_e = jnp.sum(blk[:, None] * EXP_ROWS >= pad_ends[None, :], axis=1).astype(jnp.int32)
    blk_e = jnp.minimum(blk_e, N_EXP - 1)
    blk_e = jnp.where(blk < n_valid, blk_e, blk_e[jnp.maximum(n_valid - 1, 0)])
    n_slots = n_blocks * EXP_ROWS
    tail = lambda v, last: jnp.concatenate([v, jnp.reshape(last, (1,))]).astype(jnp.int32)
    slots = _dispatch(part, h2, dest0, dest1, tail(counts, 0), tail(padded, n_slots - pad_ends[-1]),
                      tail(pad_starts, pad_ends[-1]), n_slots)
    ys = _experts(slots, blk_e, n_valid.reshape(1).astype(jnp.int32), wg, wu, wd)
    return _combine(part, ys, dest0, dest1, routes, x, mod, ng)


def _pad_cols(w, n):
    return jnp.pad(w, ((0, 0), (0, n - w.shape[1])))


def kernel(x_prompt, x_sample, state_gla, state_ssd, c, c_ctx, w_mod, b_mod, norm_g, gla_w_in, gla_w_a2, gla_b_a,
           gla_norm_g, gla_w_out, ssd_w_in, ssd_conv_w, ssd_conv_b, ssd_dt_bias, ssd_a_log, ssd_d, ssd_norm_g,
           ssd_w_out, ffn_w_gate, ffn_w_up, ffn_w_down, moe_w_router, moe_w_gate, moe_w_up, moe_w_down):
    bp, seq, _ = x_prompt.shape
    bd, dseq, _ = x_sample.shape
    parts = (_Part(bp, seq, 0, False), _Part(bd, dseq, 1, True))
    xs = [x_prompt.reshape(bp * seq, D), x_sample.reshape(bd * dseq, D)]

    c_all = jnp.concatenate([c_ctx[None], c, jnp.zeros((8 - 1 - bd, D), F32)], axis=0)
    mods = _modulation(c_all, w_mod, b_mod).reshape(DEPTH, 8, 6, D)

    gla_states, ssd_states = [], []
    for l in range(DEPTH):
        j = l // 2
        mod = mods[l]
        ng = [norm_g[l, i].reshape(1, D) for i in range(4)]
        if l % 2 == 0:
            wg, wu, wd = ffn_w_gate[j].astype(BF16), ffn_w_up[j].astype(BF16), ffn_w_down[j].astype(BF16)
            for pi, part in enumerate(parts):
                s0 = state_gla[:, j] if pi == 1 else None
                xn, h2, sfin = _gla_layer_mixer(part, xs[pi], mod, ng, gla_w_in[j], gla_w_a2[j], gla_b_a[j],
                                                gla_norm_g[j], gla_w_out[j], s0)
                if pi == 0:
                    gla_states.append(sfin)
                xs[pi] = _ffn(part, h2, xn, mod, ng[3], wg, wu, wd)
        else:
            w_router = _pad_cols(moe_w_router[j], LANES).astype(BF16)
            wg, wu, wd = moe_w_gate[j].astype(BF16), moe_w_up[j].astype(BF16), moe_w_down[j].astype(BF16)
            for pi, part in enumerate(parts):
                s0 = state_ssd[:, j] if pi == 1 else None
                xn, h2, sfin = _ssd_layer_mixer(part, xs[pi], mod, ng, ssd_w_in[j], ssd_conv_w[j], ssd_conv_b[j],
                                                ssd_dt_bias[j], ssd_a_log[j], ssd_d[j], ssd_norm_g[j], ssd_w_out[j],
                                                s0, pi == 1)
                if pi == 0:
                    ssd_states.append(sfin)
                xs[pi] = _moe(part, h2, xn, mod, ng[3], w_router, wg, wu, wd)

    y_prompt = xs[0].reshape(bp, seq, D)
    y_sample = xs[1].reshape(bd, dseq, D)
    return y_prompt, y_sample, jnp.stack(gla_states, axis=1), jnp.stack(ssd_states, axis=1)
```

```python
import functools

import jax
import jax.numpy as jnp
from jax import lax
from jax.experimental import pallas as pl
from jax.experimental.pallas import tpu as pltpu

F32 = jnp.float32
BF16 = jnp.bfloat16

D = 1024
DEPTH = 4
EPS = 1e-6
GRID_W = 64
GLA_H = 4
GLA_HK = 128
GLA_HV = 256
GLA_DK = 512
GLA_DV = 1024
GLA_RANK = 16
SSD_DI = 2048
SSD_P = 64
SSD_NH = 32
SSD_G = 4
SSD_HPG = 8
SSD_N = 128
SSD_CW = 5
SSD_XBC = SSD_DI + 2 * SSD_G * SSD_N
SSD_GW = SSD_DI // SSD_G
FFN_DIM = 2816
N_EXP = 8
EXP_DIM = 3584
CHUNK = 128
LANES = 128
EXP_ROWS = 256
EXP_FCHUNK = 512
VMEM_LIMIT = 56 * 1024 * 1024
LOG2E = 1.4426950408889634


def _cp(n_axes=1, vmem=VMEM_LIMIT):
    return pltpu.CompilerParams(dimension_semantics=("arbitrary",) * n_axes, vmem_limit_bytes=vmem)


def _dot(a, b):
    return jnp.dot(a, b, preferred_element_type=F32)


def _dot_nt(a, b):
    return lax.dot_general(a, b, (((1,), (1,)), ((), ())), preferred_element_type=F32)


def _dot_tn(a, b):
    return lax.dot_general(a, b, (((0,), (0,)), ((), ())), preferred_element_type=F32)


def _rms(x, g):
    ms = jnp.mean(x * x, axis=-1, keepdims=True)
    return x * lax.rsqrt(ms + EPS) * g


def _silu(x):
    return x * jax.nn.sigmoid(x)


def _softplus(x):
    return jnp.maximum(x, 0.0) + jnp.log1p(jnp.exp(-jnp.abs(x)))


def _hi_lo(x):
    hi = x.astype(BF16)
    lo = (x - hi.astype(F32)).astype(BF16)
    return hi, lo


def _tri_masks(c):
    row = lax.broadcasted_iota(jnp.int32, (c, c), 0)
    col = lax.broadcasted_iota(jnp.int32, (c, c), 1)
    keeps = (row >= col, row <= col)
    tris = tuple(jnp.where(k, 1.0, 0.0).astype(BF16) for k in keeps)
    return keeps, tris


def _mod_kernel(c_ref, w_ref, b_ref, o_ref):
    s = _silu(c_ref[...]).astype(BF16)
    o_ref[0] = _dot(s, w_ref[0].astype(BF16)) + b_ref[0]


def _modulation(c_all, w_mod, b_mod):
    tn = 1536
    return pl.pallas_call(
        _mod_kernel,
        grid=(DEPTH, 6 * D // tn),
        in_specs=[
            pl.BlockSpec((8, D), lambda l, n: (0, 0)),
            pl.BlockSpec((1, D, tn), lambda l, n: (l, 0, n)),
            pl.BlockSpec((1, 1, tn), lambda l, n: (l, 0, n)),
        ],
        out_specs=pl.BlockSpec((1, 8, tn), lambda l, n: (l, 0, n)),
        out_shape=jax.ShapeDtypeStruct((DEPTH, 8, 6 * D), F32),
        compiler_params=_cp(2),
        name="modulation",
    )(c_all, w_mod, b_mod.reshape(DEPTH, 1, 6 * D))


class _Part:
    def __init__(self, nseq, t, mod_base, mod_per_seq):
        self.nseq, self.t, self.m = nseq, t, nseq * t
        self.mod_base = mod_base
        self.rows_per_mod = t if mod_per_seq else nseq * t

    def mod_spec(self, tm):
        base, per = self.mod_base, self.rows_per_mod
        return pl.BlockSpec((1, 6, D), lambda i: (base + (i * tm) // per, 0, 0))


def _modulate(x, mod_ref, ng_ref, shift, scale):
    mod = mod_ref[0]
    return _rms(x, ng_ref[...]) * (1.0 + mod[scale:scale + 1]) + mod[shift:shift + 1]


GLA_SEGS = (GLA_DK, GLA_DK, GLA_DV, GLA_DV, LANES)


def _gla_in_kernel(x_ref, mod_ref, ng_ref, w_ref, q_ref, k_ref, v_ref, r_ref, a_ref):
    h = _modulate(x_ref[...], mod_ref, ng_ref, 0, 1).astype(BF16)
    off = 0
    for o_ref, n in zip((q_ref, k_ref, v_ref, r_ref, a_ref), GLA_SEGS):
        o_ref[...] = _dot(h, w_ref[:, off:off + n]).astype(o_ref.dtype)
        off += n


def _gla_in(part, x, mod, ng, w):
    tm = 512
    row = lambda i: (i, 0)
    const = lambda i: (0, 0)
    return pl.pallas_call(
        _gla_in_kernel,
        grid=(part.m // tm,),
        in_specs=[
            pl.BlockSpec((tm, D), row),
            part.mod_spec(tm),
            pl.BlockSpec((1, D), const),
            pl.BlockSpec((D, sum(GLA_SEGS)), const),
        ],
        out_specs=[pl.BlockSpec((tm, n), row) for n in GLA_SEGS],
        out_shape=[jax.ShapeDtypeStruct((part.m, n), BF16) for n in GLA_SEGS],
        compiler_params=_cp(1),
        name="gla_in",
    )(x, mod, ng, w)


INTRA_GROUP = 4


def _gla_scan_kernel(*refs, nchunks, hp, has_s0):
    if has_s0:
        q_ref, k_ref, v_ref, a_ref, wa_ref, ba_ref, s0_ref, o_ref, st_scr, qg_scr, kd_scr, dec_scr = refs
        sfin_ref = None
    else:
        q_ref, k_ref, v_ref, a_ref, wa_ref, ba_ref, o_ref, sfin_ref, st_scr, qg_scr, kd_scr, dec_scr = refs
    c = CHUNK
    keeps, tris = _tri_masks(c)
    scale = GLA_HK ** -0.5
    was = [wa_ref[d].astype(BF16) for d in (0, 1)]
    bas = [ba_ref[d] for d in (0, 1)]
    width = hp * GLA_HK

    group = INTRA_GROUP if nchunks % INTRA_GROUP == 0 else 2

    def intra(i, carry):
        cis = [i * group + j for j in range(group)]
        sls = [pl.ds(pl.multiple_of(ci * c, c), c) for ci in cis]
        streams = [(j, d) for j in range(group) for d in (0, 1)]
        a = [a_ref[sl, :] for sl in sls]
        pre = {s: _dot(a[s[0]], was[s[1]]) + bas[s[1]] for s in streams}
        split = {}
        for s in streams:
            g = (jnp.minimum(pre[s], 0.0) - jnp.log1p(jnp.exp(-jnp.abs(pre[s])))) * (1.0 / 16.0)
            split[s] = _hi_lo(g)
        b = {s: _dot(tris[s[1]], split[s][0]) + _dot(tris[s[1]], split[s][1]) for s in streams}
        q = [q_ref[sl, :].astype(F32) * scale for sl in sls]
        k = [k_ref[sl, :].astype(F32) for sl in sls]
        qg, kg = {}, {}
        for s in streams:
            j, d = s
            b_last = b[s][c - 1:c, :] if d == 0 else b[s][0:1, :]
            qg[s] = (q[j] * jnp.exp(b[s])).astype(BF16)
            kg[s] = (k[j] * jnp.exp(-b[s])).astype(BF16)
            qg_scr[d, sls[j], :] = qg[s]
            kd_scr[d, sls[j], :] = (k[j] * jnp.exp(b_last - b[s])).astype(BF16)
            dec_scr[d, pl.ds(pl.multiple_of(cis[j] * 8, 8), 8), :] = jnp.broadcast_to(jnp.exp(b_last), (8, width))
        heads = [(s, h) for s in streams for h in range(hp)]
        att = {}
        for s, h in heads:
            ks = slice(h * GLA_HK, (h + 1) * GLA_HK)
            att[s, h] = jnp.where(keeps[s[1]], _dot_nt(qg[s][:, ks], kg[s][:, ks]), 0.0).astype(BF16)
        out = {(s, h): _dot(att[s, h], v_ref[sls[s[0]], h * GLA_HV:(h + 1) * GLA_HV]) for s, h in heads}
        for j in range(group):
            for h in range(hp):
                o_ref[sls[j], h * GLA_HV:(h + 1) * GLA_HV] = out[(j, 0), h] + out[(j, 1), h]
        return carry

    lax.fori_loop(0, nchunks // group, intra, 0)

    for d in (0, 1):
        for h in range(hp):
            st_scr[d, h] = s0_ref[d, h].T if has_s0 else jnp.zeros((GLA_HV, GLA_HK), F32)

    def scan(i, carry):
        cis = (i, nchunks - 1 - i)
        sls = [pl.ds(pl.multiple_of(ci * c, c), c) for ci in cis]
        decs = [dec_scr[d, pl.ds(pl.multiple_of(cis[d] * 8, 8), 1), :] for d in (0, 1)]
        heads = [(d, h) for d in (0, 1) for h in range(hp)]
        ks = lambda h: slice(h * GLA_HK, (h + 1) * GLA_HK)
        vs = lambda h: slice(h * GLA_HV, (h + 1) * GLA_HV)
        st = {dh: st_scr[dh[0], dh[1]] for dh in heads}
        inter = {(d, h): _dot_nt(qg_scr[d, sls[d], ks(h)], st[d, h].astype(BF16)) for d, h in heads}
        grow = {(d, h): _dot_tn(v_ref[sls[d], vs(h)], kd_scr[d, sls[d], ks(h)]) for d, h in heads}
        for d, h in heads:
            o_ref[sls[d], vs(h)] += inter[d, h]
            st_scr[d, h] = st[d, h] * decs[d][:, ks(h)] + grow[d, h]
        return carry

    lax.fori_loop(0, nchunks, scan, 0, unroll=2)
    if sfin_ref is not None:
        for d in (0, 1):
            for h in range(hp):
                sfin_ref[d, h] = st_scr[d, h].T


def _gla_scan(part, q, k, v, a, wa, ba, s0):
    t = part.t
    nchunks = t // CHUNK
    assert nchunks % 2 == 0
    hp = GLA_H if t <= 512 else 2
    has_s0 = s0 is not None
    seq_head = lambda b, h: (b, h)
    state_spec = pl.BlockSpec((None, 2, hp, GLA_HK, GLA_HV), lambda b, h: (b, 0, h, 0, 0))
    in_specs = [
        pl.BlockSpec((t, hp * GLA_HK), seq_head),
        pl.BlockSpec((t, hp * GLA_HK), seq_head),
        pl.BlockSpec((t, hp * GLA_HV), seq_head),
        pl.BlockSpec((t, LANES), lambda b, h: (b, 0)),
        pl.BlockSpec((2, LANES, hp * GLA_HK), lambda b, h: (0, 0, h)),
        pl.BlockSpec((2, 1, hp * GLA_HK), lambda b, h: (0, 0, h)),
    ]
    args = [q, k, v, a, wa, ba]
    o_spec = pl.BlockSpec((t, hp * GLA_HV), seq_head)
    o_shape = jax.ShapeDtypeStruct((part.m, GLA_DV), F32)
    if has_s0:
        in_specs.append(state_spec)
        args.append(s0)
        out_specs, out_shape = o_spec, o_shape
    else:
        out_specs = [o_spec, state_spec]
        out_shape = [o_shape, jax.ShapeDtypeStruct((part.nseq, 2, GLA_H, GLA_HK, GLA_HV), F32)]
    res = pl.pallas_call(
        functools.partial(_gla_scan_kernel, nchunks=nchunks, hp=hp, has_s0=has_s0),
        grid=(part.nseq, GLA_H // hp),
        in_specs=in_specs,
        out_specs=out_specs,
        out_shape=out_shape,
        scratch_shapes=[
            pltpu.VMEM((2, hp, GLA_HV, GLA_HK), F32),
            pltpu.VMEM((2, t, hp * GLA_HK), BF16),
            pltpu.VMEM((2, t, hp * GLA_HK), BF16),
            pltpu.VMEM((2, nchunks * 8, hp * GLA_HK), F32),
        ],
        compiler_params=_cp(2),
        name="gla_scan",
    )(*args)
    return (res, None) if has_s0 else res


def _mix_out_tail(mix, x, mod_ref, ng1_ref, ng2_ref):
    mod = mod_ref[0]
    xn = x + mod[2:3] * _rms(mix, ng1_ref[...])
    h2 = _rms(xn, ng2_ref[...]) * (1.0 + mod[4:5]) + mod[3:4]
    return xn, h2


def _gla_out_kernel(o_ref, r_ref, x_ref, mod_ref, gng_ref, ng1_ref, ng2_ref, w_ref, xn_ref, h2_ref):
    o = o_ref[...]
    gng = gng_ref[...]
    on = jnp.concatenate([_rms(o[:, h * GLA_HV:(h + 1) * GLA_HV], gng) for h in range(GLA_H)], axis=1)
    on = (on * _silu(r_ref[...].astype(F32))).astype(BF16)
    xn, h2 = _mix_out_tail(_dot(on, w_ref[...]), x_ref[...], mod_ref, ng1_ref, ng2_ref)
    xn_ref[...] = xn
    h2_ref[...] = h2.astype(h2_ref.dtype)


def _gla_out(part, o, r, x, mod, gng, ng1, ng2, w):
    tm = 512
    row = lambda i: (i, 0)
    const = lambda i: (0, 0)
    return pl.pallas_call(
        _gla_out_kernel,
        grid=(part.m // tm,),
        in_specs=[
            pl.BlockSpec((tm, GLA_DV), row),
            pl.BlockSpec((tm, GLA_DV), row),
            pl.BlockSpec((tm, D), row),
            part.mod_spec(tm),
            pl.BlockSpec((1, GLA_HV), const),
            pl.BlockSpec((1, D), const),
            pl.BlockSpec((1, D), const),
            pl.BlockSpec((GLA_DV, D), const),
        ],
        out_specs=[pl.BlockSpec((tm, D), row), pl.BlockSpec((tm, D), row)],
        out_shape=[jax.ShapeDtypeStruct((part.m, D), F32), jax.ShapeDtypeStruct((part.m, D), BF16)],
        compiler_params=_cp(1),
        name="gla_out",
    )(o, r, x, mod, gng, ng1, ng2, w)


def _gla_layer_mixer(part, x, mod, ng, w_in, w_a2, b_a, gng, w_out, s0):
    w_in = _pad_cols(w_in, sum(GLA_SEGS)).astype(BF16)
    wa = jnp.zeros((2, LANES, GLA_DK), F32)
    wa = wa.at[0, :GLA_RANK].set(w_a2[0]).at[1, GLA_RANK:2 * GLA_RANK].set(w_a2[1])
    q, k, v, r, a = _gla_in(part, x, mod, ng[0], w_in)
    o, sfin = _gla_scan(part, q, k, v, a, wa, b_a.reshape(2, 1, GLA_DK), s0)
    xn, h2 = _gla_out(part, o, r, x, mod, gng.reshape(1, GLA_HV), ng[1], ng[2], w_out.astype(BF16))
    return xn, h2, sfin


def _ffn_kernel(h_ref, x_ref, mod_ref, ng_ref, wg_ref, wu_ref, wd_ref, o_ref):
    h = h_ref[...]
    hd = (_silu(_dot(h, wg_ref[...])) * _dot(h, wu_ref[...])).astype(BF16)
    ff = _dot(hd, wd_ref[...])
    o_ref[...] = x_ref[...] + mod_ref[0][5:6] * _rms(ff, ng_ref[...])


def _ffn(part, h2, x, mod, ng, wg, wu, wd):
    tm = 256
    row = lambda i: (i, 0)
    const = lambda i: (0, 0)
    once = pl.Buffered(1)
    return pl.pallas_call(
        _ffn_kernel,
        grid=(part.m // tm,),
        in_specs=[
            pl.BlockSpec((tm, D), row),
            pl.BlockSpec((tm, D), row),
            part.mod_spec(tm),
            pl.BlockSpec((1, D), const),
            pl.BlockSpec((D, FFN_DIM), const, pipeline_mode=once),
            pl.BlockSpec((D, FFN_DIM), const, pipeline_mode=once),
            pl.BlockSpec((FFN_DIM, D), const, pipeline_mode=once),
        ],
        out_specs=pl.BlockSpec((tm, D), row),
        out_shape=jax.ShapeDtypeStruct((part.m, D), F32),
        compiler_params=_cp(1),
        name="ffn",
    )(h2, x, mod, ng, wg, wu, wd)


SSD_NCOL = 8


def _grid_cols_spec(rows=GRID_W):
    per_b = GRID_W // SSD_NCOL
    return pl.BlockSpec((None, rows, SSD_NCOL, D), lambda i: (i // per_b, 0, i % per_b, 0))


def _read_colmajor(x_ref):
    return jnp.concatenate([x_ref[:, c, :] for c in range(SSD_NCOL)], axis=0)


def _write_colmajor(o_ref, val):
    for c in range(SSD_NCOL):
        o_ref[:, c, :] = val[c * GRID_W:(c + 1) * GRID_W, :].astype(o_ref.dtype)


def _ssd_in_kernel(x_ref, xp_ref, xn_ref, mod_ref, ng_ref, wz_ref, wx_ref, wdt_ref, cw_ref, cb_ref,
                   z_ref, xc_ref, bc_ref, cc_ref, dt_ref, p_scr, *, colmajor, blocks_per_seq, tm):
    i = pl.program_id(0)
    first = (i % blocks_per_seq) == 0
    last = (i % blocks_per_seq) == blocks_per_seq - 1
    if colmajor:
        x = _read_colmajor(x_ref)
        xp = xp_ref[:, SSD_NCOL - 1, :]
        xn = xn_ref[:, 0, :]
    else:
        x, xp, xn = x_ref[...], xp_ref[...], xn_ref[...]
    xcat = jnp.concatenate([xp, x, xn], axis=0)
    h = _modulate(xcat, mod_ref, ng_ref, 0, 1).astype(BF16)
    hm = h[8:8 + tm]
    z_ref[...] = _dot(hm, wz_ref[...]).astype(z_ref.dtype)
    dt_ref[...] = _dot(hm, wdt_ref[...])
    cw = cw_ref[...]
    cb = cb_ref[...]
    wcol = 512
    rb = 128
    for cc in range(SSD_XBC // wcol):
        buf = p_scr.at[cc % 2]
        buf[...] = _dot(h, wx_ref[:, cc * wcol:(cc + 1) * wcol])
        buf[0:8, :] = jnp.where(first, 0.0, buf[0:8, :])
        buf[tm + 8:tm + 16, :] = jnp.where(last, 0.0, buf[tm + 8:tm + 16, :])
        out_ref, col0 = (xc_ref, cc * wcol) if cc < 4 else ((bc_ref, 0) if cc == 4 else (cc_ref, 0))
        for r0 in range(0, tm, rb):
            for l0 in range(0, wcol, LANES):
                src = slice(cc * wcol + l0, cc * wcol + l0 + LANES)
                tile = buf[r0:r0 + rb + 16, l0:l0 + LANES]
                acc = cw[0:1, src] * tile[6:6 + rb]
                for j in range(1, SSD_CW):
                    acc = acc + cw[j:j + 1, src] * tile[6 + j:6 + j + rb]
                y = _silu(acc + cb[:, src])
                out_ref[r0:r0 + rb, col0 + l0:col0 + l0 + LANES] = y.astype(out_ref.dtype)


def _ssd_in(part, x, mod, ng, wz, wx, wdt, cw, cb, colmajor):
    m = part.m
    const = lambda i: (0, 0)
    if colmajor:
        tm = SSD_NCOL * GRID_W
        per_b = GRID_W // SSD_NCOL
        xv = x.reshape(part.nseq, GRID_W, GRID_W, D)
        x_specs = [
            _grid_cols_spec(),
            pl.BlockSpec((None, 8, SSD_NCOL, D), lambda i: (i // per_b, GRID_W // 8 - 1,
                                                            jnp.maximum(i % per_b - 1, 0), 0)),
            pl.BlockSpec((None, 8, SSD_NCOL, D), lambda i: (i // per_b, 0, jnp.minimum(i % per_b + 1, per_b - 1), 0)),
        ]
        xs = (xv, xv, xv)
    else:
        tm = part.t
        nb8 = m // 8
        x_specs = [
            pl.BlockSpec((tm, D), lambda i: (i, 0)),
            pl.BlockSpec((8, D), lambda i: (jnp.maximum(i * (tm // 8) - 1, 0), 0)),
            pl.BlockSpec((8, D), lambda i: (jnp.minimum((i + 1) * (tm // 8), nb8 - 1), 0)),
        ]
        xs = (x, x, x)
    row = lambda i: (i, 0)
    widths = (SSD_DI, SSD_DI, SSD_G * SSD_N, SSD_G * SSD_N, LANES)
    dts = (BF16, BF16, BF16, BF16, F32)
    return pl.pallas_call(
        functools.partial(_ssd_in_kernel, colmajor=colmajor, blocks_per_seq=part.t // tm, tm=tm),
        grid=(m // tm,),
        in_specs=x_specs + [
            part.mod_spec(tm),
            pl.BlockSpec((1, D), const),
            pl.BlockSpec((D, SSD_DI), const),
            pl.BlockSpec((D, SSD_XBC), const),
            pl.BlockSpec((D, LANES), const),
            pl.BlockSpec((SSD_CW, SSD_XBC), const),
            pl.BlockSpec((1, SSD_XBC), const),
        ],
        out_specs=[pl.BlockSpec((tm, w), row) for w in widths],
        out_shape=[jax.ShapeDtypeStruct((m, w), dt) for w, dt in zip(widths, dts)],
        scratch_shapes=[pltpu.VMEM((2, tm + 16, 512), F32)],
        compiler_params=_cp(1),
        name="ssd_in",
    )(*xs, mod, ng, wz, wx, wdt, cw, cb)


SSD_PAIRS = SSD_HPG // 2


def _ssd_scan_kernel(*refs, nchunks, has_s0):
    if has_s0:
        (z_ref, x_ref, b_ref, c_ref, dt_ref, dtb_ref, alog_ref, dsk_ref, ng_ref, s0_ref,
         y_ref, h_scr, yacc_scr) = refs
        sfin_ref = None
    else:
        (z_ref, x_ref, b_ref, c_ref, dt_ref, dtb_ref, alog_ref, dsk_ref, ng_ref,
         y_ref, sfin_ref, h_scr, yacc_scr) = refs
    c = CHUNK
    grp = pl.program_id(1)
    shift = (LANES - 2 * SSD_HPG * grp) % LANES
    keeps, tris = _tri_masks(c)
    lane = lax.broadcasted_iota(jnp.int32, (1, LANES), 1)
    low = lane < SSD_P
    m_lo = jnp.where(low, 1.0, 0.0).astype(BF16)
    m_hi = jnp.where(low, 0.0, 1.0).astype(BF16)
    a_row = -jnp.exp(alog_ref[...])
    dtb = dtb_ref[...]
    for d in (0, 1):
        for pp in range(SSD_PAIRS):
            if has_s0:
                h_scr[d, pp] = jnp.concatenate([s0_ref[d, 2 * pp], s0_ref[d, 2 * pp + 1]], axis=1)
            else:
                h_scr[d, pp] = jnp.zeros((SSD_N, 2 * SSD_P), F32)

    def step(i, final):
        dirs = (0, 1)
        cis = (i, nchunks - 1 - i)
        sls = [pl.ds(pl.multiple_of(ci * c, c), c) for ci in cis]
        dt = [_softplus(pltpu.roll(dt_ref[sls[d], :], shift, 1) + dtb) for d in dirs]
        split = [_hi_lo(dt[d] * a_row) for d in dirs]
        bm = [b_ref[sls[d], :] for d in dirs]
        cm = [c_ref[sls[d], :] for d in dirs]
        xb = [x_ref[sls[d], :] for d in dirs]
        cum = [(_dot(tris[d], split[d][0]) + _dot(tris[d], split[d][1])) * LOG2E for d in dirs]
        cb = [_dot_nt(cm[d], bm[d]) for d in dirs]
        log_dt = [jnp.log(dt[d]) * LOG2E for d in dirs]
        total = [cum[0][c - 1:c, :], cum[1][0:1, :]]
        f_in = [jnp.exp2(total[d] - cum[d] + log_dt[d]) for d in dirs]
        e_tot = [jnp.exp2(total[d]) for d in dirs]
        src_t = [(cum[d] - log_dt[d]).T for d in dirs]
        f_t = [f_in[d].T for d in dirs]
        bm_t32 = [bm[d].astype(F32).T for d in dirs]
        cm32 = [cm[d].astype(F32) for d in dirs]
        ys = {}
        for pp in range(SSD_PAIRS):
            for d in dirs:
                xp = xb[d][:, pp * LANES:(pp + 1) * LANES]
                x2 = jnp.concatenate([xp * m_lo, xp * m_hi], axis=0)
                hst = h_scr[d, pp]
                hb = hst.astype(BF16)
                ws, cs, bs = [], [], []
                for s in (0, 1):
                    l = d * SSD_HPG + 2 * pp + s
                    ccol = jnp.broadcast_to(cum[d][:, l:l + 1], (c, c))
                    w = jnp.exp2(jnp.where(keeps[d], ccol - src_t[d][l:l + 1, :], -jnp.inf)) * cb[d]
                    ws.append(w.astype(BF16))
                    cs.append((cm32[d] * jnp.exp2(ccol)).astype(BF16))
                    bs.append((bm_t32[d] * f_t[d][l:l + 1, :]).astype(BF16))
                lhs = jnp.concatenate(ws + cs, axis=1)
                rhs = jnp.concatenate([x2, hb * m_lo, hb * m_hi], axis=0)
                ys[d, pp] = _dot(lhs, rhs)
                l0 = d * SSD_HPG + 2 * pp
                dec = jnp.where(low, e_tot[d][:, l0:l0 + 1], e_tot[d][:, l0 + 1:l0 + 2])
                h_scr[d, pp] = hst * dec + _dot(jnp.concatenate(bs, axis=1), x2)
        for d in dirs:
            y = jnp.concatenate([ys[d, pp] for pp in range(SSD_PAIRS)], axis=1)
            if final:
                y = y + yacc_scr[sls[d], :] + dsk_ref[...] * xb[d].astype(F32)
                y = y * _silu(z_ref[sls[d], :].astype(F32))
                y_ref[sls[d], :] = _rms(y, ng_ref[...]).astype(y_ref.dtype)
            else:
                yacc_scr[sls[d], :] = y

    half = nchunks // 2

    def first_half(i, carry):
        step(i, False)
        return carry

    def second_half(i, carry):
        step(i, True)
        return carry

    unroll = 2 if half % 2 == 0 else 1
    lax.fori_loop(0, half, first_half, 0, unroll=unroll)
    lax.fori_loop(half, nchunks, second_half, 0, unroll=unroll)
    if sfin_ref is not None:
        for d in (0, 1):
            for pp in range(SSD_PAIRS):
                hst = h_scr[d, pp]
                sfin_ref[d, 2 * pp] = hst[:, :SSD_P]
                sfin_ref[d, 2 * pp + 1] = hst[:, SSD_P:]


def _ssd_scan(part, z, xc, bc, cc, dt, dtb, alog, dsk, ng, s0):
    t = part.t
    nchunks = t // CHUNK
    assert nchunks % 2 == 0
    has_s0 = s0 is not None
    sg = lambda b, g: (b, g)
    grp_row = lambda b, g: (g, 0, 0)
    state_spec = pl.BlockSpec((None, 2, SSD_HPG, SSD_N, SSD_P), lambda b, g: (b, 0, g, 0, 0))
    in_specs = [
        pl.BlockSpec((t, SSD_GW), sg),
        pl.BlockSpec((t, SSD_GW), sg),
        pl.BlockSpec((t, SSD_N), sg),
        pl.BlockSpec((t, SSD_N), sg),
        pl.BlockSpec((t, LANES), lambda b, g: (b, 0)),
        pl.BlockSpec((None, 1, LANES), grp_row),
        pl.BlockSpec((None, 1, LANES), grp_row),
        pl.BlockSpec((None, 1, SSD_GW), grp_row),
        pl.BlockSpec((None, 1, SSD_GW), grp_row),
    ]
    args = [z, xc, bc, cc, dt, dtb, alog, dsk, ng]
    y_spec = pl.BlockSpec((t, SSD_GW), sg)
    y_shape = jax.ShapeDtypeStruct((part.m, SSD_DI), BF16)
    if has_s0:
        in_specs.append(state_spec)
        args.append(s0)
        out_specs, out_shape = y_spec, y_shape
    else:
        out_specs = [y_spec, state_spec]
        out_shape = [y_shape, jax.ShapeDtypeStruct((part.nseq, 2, SSD_NH, SSD_N, SSD_P), F32)]
    res = pl.pallas_call(
        functools.partial(_ssd_scan_kernel, nchunks=nchunks, has_s0=has_s0),
        grid=(part.nseq, SSD_G),
        in_specs=in_specs,
        out_specs=out_specs,
        out_shape=out_shape,
        scratch_shapes=[pltpu.VMEM((2, SSD_PAIRS, SSD_N, 2 * SSD_P), F32), pltpu.VMEM((t, SSD_GW), F32)],
        compiler_params=_cp(2),
        name="ssd_scan",
    )(*args)
    return (res, None) if has_s0 else res


def _ssd_out_kernel(y_ref, x_ref, mod_ref, ng1_ref, ng2_ref, w_ref, xn_ref, h2_ref, *, colmajor):
    x = _read_colmajor(x_ref) if colmajor else x_ref[...]
    xn, h2 = _mix_out_tail(_dot(y_ref[...], w_ref[...]), x, mod_ref, ng1_ref, ng2_ref)
    if colmajor:
        _write_colmajor(xn_ref, xn)
        _write_colmajor(h2_ref, h2)
    else:
        xn_ref[...] = xn
        h2_ref[...] = h2


def _ssd_out(part, y, x, mod, ng1, ng2, w, colmajor):
    m = part.m
    const = lambda i: (0, 0)
    row = lambda i: (i, 0)
    if colmajor:
        tm = SSD_NCOL * GRID_W
        vshape = (part.nseq, GRID_W, GRID_W, D)
        xspec = _grid_cols_spec()
        x = x.reshape(vshape)
    else:
        tm = 512
        vshape = (m, D)
        xspec = pl.BlockSpec((tm, D), row)
    xn, h2 = pl.pallas_call(
        functools.partial(_ssd_out_kernel, colmajor=colmajor),
        grid=(m // tm,),
        in_specs=[
            pl.BlockSpec((tm, SSD_DI), row),
            xspec,
            part.mod_spec(tm),
            pl.BlockSpec((1, D), const),
            pl.BlockSpec((1, D), const),
            pl.BlockSpec((SSD_DI, D), const),
        ],
        out_specs=[xspec, xspec],
        out_shape=[jax.ShapeDtypeStruct(vshape, F32), jax.ShapeDtypeStruct(vshape, F32)],
        compiler_params=_cp(1),
        name="ssd_out",
    )(y, x, mod, ng1, ng2, w)
    return xn.reshape(m, D), h2.reshape(m, D)


def _ssd_layer_mixer(part, x, mod, ng, w_in, conv_w, conv_b, dt_bias, a_log, d_skip, norm_g, w_out, s0, colmajor):
    wz = w_in[:, :SSD_DI].astype(BF16)
    wx = w_in[:, SSD_DI:SSD_DI + SSD_XBC].astype(BF16)
    wdt = w_in[:, SSD_DI + SSD_XBC:].reshape(D, 2, SSD_G, SSD_HPG).transpose(0, 2, 1, 3)
    wdt = _pad_cols(wdt.reshape(D, 2 * SSD_NH), LANES).astype(BF16)
    per_grp = lambda p: _pad_cols(p.reshape(2, SSD_G, SSD_HPG).transpose(1, 0, 2).reshape(SSD_G, 2 * SSD_HPG),
                                  LANES).reshape(SSD_G, 1, LANES)
    dsk = jnp.repeat(d_skip, SSD_P).reshape(SSD_G, 1, SSD_GW)
    sng = norm_g.reshape(SSD_G, 1, SSD_GW)
    z, xc, bc, cc, dt = _ssd_in(part, x, mod, ng[0], wz, wx, wdt, conv_w, conv_b.reshape(1, SSD_XBC), colmajor)
    y, sfin = _ssd_scan(part, z, xc, bc, cc, dt, per_grp(dt_bias), per_grp(a_log), dsk, sng, s0)
    xn, h2 = _ssd_out(part, y, x, mod, ng[1], ng[2], w_out.astype(BF16), colmajor)
    return xn, h2, sfin


ROUTER_TM = 512
R_E0, R_E1, R_G0, R_G1, R_RANK0, R_RANK1 = range(6)


def _router_kernel(h_ref, w_ref, r_ref, cnt_ref, carry_scr):
    tm = ROUTER_TM

    @pl.when(pl.program_id(0) == 0)
    def _():
        carry_scr[...] = jnp.zeros_like(carry_scr)

    lane = lax.broadcasted_iota(jnp.int32, (tm, LANES), 1).astype(F32)
    logits = jnp.where(lane < N_EXP, _dot(h_ref[...].astype(BF16), w_ref[...]), -jnp.inf)
    m0 = jnp.max(logits, axis=1, keepdims=True)
    i0 = jnp.min(jnp.where(logits == m0, lane, float(LANES)), axis=1, keepdims=True)
    pick0 = lane == i0
    rest = jnp.where(pick0, -jnp.inf, logits)
    m1 = jnp.max(rest, axis=1, keepdims=True)
    i1 = jnp.min(jnp.where(rest == m1, lane, float(LANES)), axis=1, keepdims=True)
    pick1 = lane == i1
    e1 = jnp.exp(m1 - m0)
    den = 1.0 + e1
    oh0 = jnp.where(pick0, 1.0, 0.0)
    oh1 = jnp.where(pick1, 1.0, 0.0)
    oh = oh0 + oh1
    r_i = lax.broadcasted_iota(jnp.int32, (tm, tm), 0)
    c_i = lax.broadcasted_iota(jnp.int32, (tm, tm), 1)
    before = jnp.where(r_i > c_i, 1.0, 0.0).astype(BF16)
    base = _dot(before, oh.astype(BF16)) + carry_scr[...]
    rank0 = jnp.sum(base * oh0, axis=1, keepdims=True)
    rank1 = jnp.sum(base * oh1, axis=1, keepdims=True)
    carry_scr[...] += jnp.sum(oh, axis=0, keepdims=True)
    out = jnp.zeros((tm, LANES), F32)
    for idx, val in ((R_E0, i0), (R_E1, i1), (R_G0, 1.0 / den), (R_G1, e1 / den),
                     (R_RANK0, rank0), (R_RANK1, rank1)):
        out = jnp.where(lane == idx, val, out)
    r_ref[...] = out
    cnt_ref[...] = jnp.broadcast_to(carry_scr[...], (8, LANES))


def _router(part, h2, w):
    tm = ROUTER_TM
    return pl.pallas_call(
        _router_kernel,
        grid=(part.m // tm,),
        in_specs=[pl.BlockSpec((tm, D), lambda i: (i, 0)), pl.BlockSpec((D, LANES), lambda i: (0, 0))],
        out_specs=[pl.BlockSpec((tm, LANES), lambda i: (i, 0)), pl.BlockSpec((8, LANES), lambda i: (0, 0))],
        out_shape=[jax.ShapeDtypeStruct((part.m, LANES), F32), jax.ShapeDtypeStruct((8, LANES), F32)],
        scratch_shapes=[pltpu.VMEM((1, LANES), F32)],
        compiler_params=_cp(1),
        name="router",
    )(h2, w)


MOE_TM = 256
DMA_UNROLL = 8


def _row_copy(src_ref, src_row, dst_ref, dst_row, sem):
    return pltpu.make_async_copy(src_ref.at[pl.ds(src_row, 1), :], dst_ref.at[pl.ds(dst_row, 1), :], sem)


def _for_rows(n, fn):
    def body(g, carry):
        for u in range(DMA_UNROLL):
            fn(g * DMA_UNROLL + u, u % 2)
        return carry

    lax.fori_loop(0, n // DMA_UNROLL, body, 0)


def _dispatch_kernel(cnt_ref, pad_ref, start_ref, d0_ref, d1_ref, h_ref, slots_ref, zero_scr, sem, zsem):
    tm = MOE_TM

    def start(r, prio):
        _row_copy(h_ref, r, slots_ref, d0_ref[0, 0, r], sem).start(priority=prio)
        _row_copy(h_ref, r, slots_ref, d1_ref[0, 0, r], sem).start(priority=prio)

    def wait(r, prio):
        _row_copy(h_ref, 0, slots_ref, 0, sem).wait()
        _row_copy(h_ref, 0, slots_ref, 0, sem).wait()

    _for_rows(tm, start)

    @pl.when(pl.program_id(0) == 0)
    def _():
        zero_scr[...] = jnp.zeros_like(zero_scr)
        for e in range(N_EXP + 1):
            first = start_ref[e]

            def fill(r, carry, first=first):
                _row_copy(zero_scr, 0, slots_ref, first + r, zsem).start()
                return carry

            def fill_wait(r, carry):
                _row_copy(zero_scr, 0, slots_ref, 0, zsem).wait()
                return carry

            lax.fori_loop(cnt_ref[e], pad_ref[e], fill, 0)
            lax.fori_loop(cnt_ref[e], pad_ref[e], fill_wait, 0)

    _for_rows(tm, wait)


def _dispatch(part, h2, dest0, dest1, counts, padded, pad_starts, n_slots):
    tm = MOE_TM
    nb = part.m // tm
    idx_spec = pl.BlockSpec((1, 1, tm), lambda i, *_: (i, 0, 0), memory_space=pltpu.SMEM)
    grid_spec = pltpu.PrefetchScalarGridSpec(
        num_scalar_prefetch=3,
        grid=(nb,),
        in_specs=[idx_spec, idx_spec, pl.BlockSpec((tm, D), lambda i, *_: (i, 0))],
        out_specs=pl.BlockSpec(memory_space=pl.ANY),
        scratch_shapes=[pltpu.VMEM((8, D), F32), pltpu.SemaphoreType.DMA(()), pltpu.SemaphoreType.DMA(())],
    )
    return pl.pallas_call(
        _dispatch_kernel,
        grid_spec=grid_spec,
        out_shape=jax.ShapeDtypeStruct((n_slots, D), F32),
        compiler_params=_cp(1),
        name="moe_dispatch",
    )(counts, padded, pad_starts, dest0.reshape(nb, 1, tm), dest1.reshape(nb, 1, tm), h2)


def _expert_kernel(be_ref, nv_ref, xs_ref, wg_ref, wu_ref, wd_ref, ys_ref):
    del be_ref
    used = pl.program_id(0) < nv_ref[0]

    @pl.when(jnp.logical_not(used))
    def _():
        ys_ref[...] = jnp.zeros_like(ys_ref)

    @pl.when(used)
    def _():
        x = xs_ref[...].astype(BF16)
        acc = None
        for f in range(EXP_DIM // EXP_FCHUNK):
            fs = slice(f * EXP_FCHUNK, (f + 1) * EXP_FCHUNK)
            hd = (_silu(_dot(x, wg_ref[:, fs])) * _dot(x, wu_ref[:, fs])).astype(BF16)
            part = _dot(hd, wd_ref[fs, :])
            acc = part if acc is None else acc + part
        ys_ref[...] = acc


def _experts(slots, blk_e, n_valid, wg, wu, wd):
    n_blocks = slots.shape[0] // EXP_ROWS
    rows = lambda j, be, nv: (j, 0)
    grid_spec = pltpu.PrefetchScalarGridSpec(
        num_scalar_prefetch=2,
        grid=(n_blocks,),
        in_specs=[
            pl.BlockSpec((EXP_ROWS, D), rows),
            pl.BlockSpec((None, D, EXP_DIM), lambda j, be, nv: (be[j], 0, 0)),
            pl.BlockSpec((None, D, EXP_DIM), lambda j, be, nv: (be[j], 0, 0)),
            pl.BlockSpec((None, EXP_DIM, D), lambda j, be, nv: (be[j], 0, 0)),
        ],
        out_specs=pl.BlockSpec((EXP_ROWS, D), rows),
    )
    return pl.pallas_call(
        _expert_kernel,
        grid_spec=grid_spec,
        out_shape=jax.ShapeDtypeStruct(slots.shape, F32),
        compiler_params=_cp(1, 60 * 1024 * 1024),
        name="moe_experts",
    )(blk_e, n_valid, slots, wg, wu, wd)


def _combine_kernel(d0_ref, d1_ref, n0_ref, n1_ref, r_ref, x_ref, mod_ref, ng_ref, ys_ref, o_ref,
                    y0_scr, y1_scr, sem):
    tm = MOE_TM
    i = pl.program_id(0)
    nb = pl.num_programs(0)
    slot = i % 2

    def gather(i0_ref, i1_ref, s):
        def start(r, prio):
            _row_copy(ys_ref, i0_ref[0, 0, r], y0_scr.at[s], r, sem.at[s]).start(priority=prio)
            _row_copy(ys_ref, i1_ref[0, 0, r], y1_scr.at[s], r, sem.at[s]).start(priority=prio)

        _for_rows(tm, start)

    @pl.when(i == 0)
    def _():
        gather(d0_ref, d1_ref, 0)

    @pl.when(i + 1 < nb)
    def _():
        gather(n0_ref, n1_ref, 1 - slot)

    def wait(r, prio):
        _row_copy(ys_ref, 0, y0_scr.at[slot], 0, sem.at[slot]).wait()
        _row_copy(ys_ref, 0, y1_scr.at[slot], 0, sem.at[slot]).wait()

    _for_rows(tm, wait)
    rt = r_ref[...]
    ff = rt[:, R_G0:R_G0 + 1] * y0_scr[slot] + rt[:, R_G1:R_G1 + 1] * y1_scr[slot]
    o_ref[...] = x_ref[...] + mod_ref[0][5:6] * _rms(ff, ng_ref[...])


def _combine(part, ys, dest0, dest1, routes, x, mod, ng):
    tm = MOE_TM
    nb = part.m // tm
    idx_spec = pl.BlockSpec((1, 1, tm), lambda i: (i, 0, 0), memory_space=pltpu.SMEM)
    nxt_spec = pl.BlockSpec((1, 1, tm), lambda i: (jnp.minimum(i + 1, nb - 1), 0, 0), memory_space=pltpu.SMEM)
    row = lambda i: (i, 0)
    d0 = dest0.reshape(nb, 1, tm)
    d1 = dest1.reshape(nb, 1, tm)
    return pl.pallas_call(
        _combine_kernel,
        grid=(nb,),
        in_specs=[
            idx_spec, idx_spec, nxt_spec, nxt_spec,
            pl.BlockSpec((tm, LANES), row),
            pl.BlockSpec((tm, D), row),
            part.mod_spec(tm),
            pl.BlockSpec((1, D), lambda i: (0, 0)),
            pl.BlockSpec(memory_space=pl.ANY),
        ],
        out_specs=pl.BlockSpec((tm, D), row),
        out_shape=jax.ShapeDtypeStruct((part.m, D), F32),
        scratch_shapes=[pltpu.VMEM((2, tm, D), F32), pltpu.VMEM((2, tm, D), F32), pltpu.SemaphoreType.DMA((2,))],
        compiler_params=_cp(1),
        name="moe_combine",
    )(d0, d1, d0, d1, routes, x, mod, ng, ys)


def _moe(part, h2, x, mod, ng, w_router, wg, wu, wd):
    routes, cnt = _router(part, h2, w_router)
    n_asg = 2 * part.m
    n_blocks = n_asg // EXP_ROWS + N_EXP
    counts = cnt[0, :N_EXP].astype(jnp.int32)
    padded = (counts + EXP_ROWS - 1) // EXP_ROWS * EXP_ROWS
    pad_ends = jnp.cumsum(padded)
    pad_starts = pad_ends - padded
    e0 = routes[:, R_E0].astype(jnp.int32)
    e1 = routes[:, R_E1].astype(jnp.int32)
    dest0 = pad_starts[e0] + routes[:, R_RANK0].astype(jnp.int32)
    dest1 = pad_starts[e1] + routes[:, R_RANK1].astype(jnp.int32)
    n_valid = pad_ends[-1] // EXP_ROWS
    blk = jnp.arange(n_blocks, dtype=jnp.int32)
    blk_e = jnp.sum(blk[:, None] * EXP_ROWS >= pad_ends[None, :], axis=1).astype(jnp.int32)
    blk_e = jnp.minimum(blk_e, N_EXP - 1)
    blk_e = jnp.where(blk < n_valid, blk_e, blk_e[jnp.maximum(n_valid - 1, 0)])
    n_slots = n_blocks * EXP_ROWS
    tail = lambda v, last: jnp.concatenate([v, jnp.reshape(last, (1,))]).astype(jnp.int32)
    slots = _dispatch(part, h2, dest0, dest1, tail(counts, 0), tail(padded, n_slots - pad_ends[-1]),
                      tail(pad_starts, pad_ends[-1]), n_slots)
    ys = _experts(slots, blk_e, n_valid.reshape(1).astype(jnp.int32), wg, wu, wd)
    return _combine(part, ys, dest0, dest1, routes, x, mod, ng)


def _pad_cols(w, n):
    return jnp.pad(w, ((0, 0), (0, n - w.shape[1])))


def kernel(x_prompt, x_sample, state_gla, state_ssd, c, c_ctx, w_mod, b_mod, norm_g, gla_w_in, gla_w_a2, gla_b_a,
           gla_norm_g, gla_w_out, ssd_w_in, ssd_conv_w, ssd_conv_b, ssd_dt_bias, ssd_a_log, ssd_d, ssd_norm_g,
           ssd_w_out, ffn_w_gate, ffn_w_up, ffn_w_down, moe_w_router, moe_w_gate, moe_w_up, moe_w_down):
    bp, seq, _ = x_prompt.shape
    bd, dseq, _ = x_sample.shape
    parts = (_Part(bp, seq, 0, False), _Part(bd, dseq, 1, True))
    xs = [x_prompt.reshape(bp * seq, D), x_sample.reshape(bd * dseq, D)]

    c_all = jnp.concatenate([c_ctx[None], c, jnp.zeros((8 - 1 - bd, D), F32)], axis=0)
    mods = _modulation(c_all, w_mod, b_mod).reshape(DEPTH, 8, 6, D)

    gla_states, ssd_states = [], []
    for l in range(DEPTH):
        j = l // 2
        mod = mods[l]
        ng = [norm_g[l, i].reshape(1, D) for i in range(4)]
        if l % 2 == 0:
            wg, wu, wd = ffn_w_gate[j].astype(BF16), ffn_w_up[j].astype(BF16), ffn_w_down[j].astype(BF16)
            for pi, part in enumerate(parts):
                s0 = state_gla[:, j] if pi == 1 else None
                xn, h2, sfin = _gla_layer_mixer(part, xs[pi], mod, ng, gla_w_in[j], gla_w_a2[j], gla_b_a[j],
                                                gla_norm_g[j], gla_w_out[j], s0)
                if pi == 0:
                    gla_states.append(sfin)
                xs[pi] = _ffn(part, h2, xn, mod, ng[3], wg, wu, wd)
        else:
            w_router = _pad_cols(moe_w_router[j], LANES).astype(BF16)
            wg, wu, wd = moe_w_gate[j].astype(BF16), moe_w_up[j].astype(BF16), moe_w_down[j].astype(BF16)
            for pi, part in enumerate(parts):
                s0 = state_ssd[:, j] if pi == 1 else None
                xn, h2, sfin = _ssd_layer_mixer(part, xs[pi], mod, ng, ssd_w_in[j], ssd_conv_w[j], ssd_conv_b[j],
                                                ssd_dt_bias[j], ssd_a_log[j], ssd_d[j], ssd_norm_g[j], ssd_w_out[j],
                                                s0, pi == 1)
                if pi == 0:
                    ssd_states.append(sfin)
                xs[pi] = _moe(part, h2, xn, mod, ng[3], w_router, wg, wu, wd)

    y_prompt = xs[0].reshape(bp, seq, D)
    y_sample = xs[1].reshape(bd, dseq, D)
    return y_prompt, y_sample, jnp.stack(gla_states, axis=1), jnp.stack(ssd_states, axis=1)
```

```python
import functools

import jax
import jax.numpy as jnp
from jax import lax
from jax.experimental import pallas as pl
from jax.experimental.pallas import tpu as pltpu

F32 = jnp.float32
BF16 = jnp.bfloat16

D = 1024
DEPTH = 4
EPS = 1e-6
GRID_W = 64
GLA_H = 4
GLA_HK = 128
GLA_HV = 256
GLA_DK = 512
GLA_DV = 1024
GLA_RANK = 16
SSD_DI = 2048
SSD_P = 64
SSD_NH = 32
SSD_G = 4
SSD_HPG = 8
SSD_N = 128
SSD_CW = 5
SSD_XBC = SSD_DI + 2 * SSD_G * SSD_N
SSD_GW = SSD_DI // SSD_G
FFN_DIM = 2816
N_EXP = 8
EXP_DIM = 3584
CHUNK = 128
LANES = 128
EXP_ROWS = 256
EXP_FCHUNK = 512
VMEM_LIMIT = 56 * 1024 * 1024
LOG2E = 1.4426950408889634


def _cp(n_axes=1, vmem=VMEM_LIMIT):
    return pltpu.CompilerParams(dimension_semantics=("arbitrary",) * n_axes, vmem_limit_bytes=vmem)


def _dot(a, b):
    return jnp.dot(a, b, preferred_element_type=F32)


def _dot_nt(a, b):
    return lax.dot_general(a, b, (((1,), (1,)), ((), ())), preferred_element_type=F32)


def _dot_tn(a, b):
    return lax.dot_general(a, b, (((0,), (0,)), ((), ())), preferred_element_type=F32)


def _rms(x, g):
    ms = jnp.mean(x * x, axis=-1, keepdims=True)
    return x * lax.rsqrt(ms + EPS) * g


def _silu(x):
    return x * jax.nn.sigmoid(x)


def _softplus(x):
    return jnp.maximum(x, 0.0) + jnp.log1p(jnp.exp(-jnp.abs(x)))


def _hi_lo(x):
    hi = x.astype(BF16)
    lo = (x - hi.astype(F32)).astype(BF16)
    return hi, lo


def _tri_masks(c):
    row = lax.broadcasted_iota(jnp.int32, (c, c), 0)
    col = lax.broadcasted_iota(jnp.int32, (c, c), 1)
    keeps = (row >= col, row <= col)
    tris = tuple(jnp.where(k, 1.0, 0.0).astype(BF16) for k in keeps)
    return keeps, tris


def _mod_kernel(c_ref, w_ref, b_ref, o_ref):
    s = _silu(c_ref[...]).astype(BF16)
    o_ref[0] = _dot(s, w_ref[0].astype(BF16)) + b_ref[0]


def _modulation(c_all, w_mod, b_mod):
    tn = 1536
    return pl.pallas_call(
        _mod_kernel,
        grid=(DEPTH, 6 * D // tn),
        in_specs=[
            pl.BlockSpec((8, D), lambda l, n: (0, 0)),
            pl.BlockSpec((1, D, tn), lambda l, n: (l, 0, n)),
            pl.BlockSpec((1, 1, tn), lambda l, n: (l, 0, n)),
        ],
        out_specs=pl.BlockSpec((1, 8, tn), lambda l, n: (l, 0, n)),
        out_shape=jax.ShapeDtypeStruct((DEPTH, 8, 6 * D), F32),
        compiler_params=_cp(2),
        name="modulation",
    )(c_all, w_mod, b_mod.reshape(DEPTH, 1, 6 * D))


class _Part:
    def __init__(self, nseq, t, mod_base, mod_per_seq):
        self.nseq, self.t, self.m = nseq, t, nseq * t
        self.mod_base = mod_base
        self.rows_per_mod = t if mod_per_seq else nseq * t

    def mod_spec(self, tm):
        base, per = self.mod_base, self.rows_per_mod
        return pl.BlockSpec((1, 6, D), lambda i: (base + (i * tm) // per, 0, 0))


def _modulate(x, mod_ref, ng_ref, shift, scale):
    mod = mod_ref[0]
    return _rms(x, ng_ref[...]) * (1.0 + mod[scale:scale + 1]) + mod[shift:shift + 1]


GLA_SEGS = (GLA_DK, GLA_DK, GLA_DV, GLA_DV, LANES)


def _gla_in_kernel(x_ref, mod_ref, ng_ref, w_ref, q_ref, k_ref, v_ref, r_ref, a_ref):
    h = _modulate(x_ref[...], mod_ref, ng_ref, 0, 1).astype(BF16)
    off = 0
    for o_ref, n in zip((q_ref, k_ref, v_ref, r_ref, a_ref), GLA_SEGS):
        o_ref[...] = _dot(h, w_ref[:, off:off + n]).astype(o_ref.dtype)
        off += n


def _gla_in(part, x, mod, ng, w):
    tm = 512
    row = lambda i: (i, 0)
    const = lambda i: (0, 0)
    return pl.pallas_call(
        _gla_in_kernel,
        grid=(part.m // tm,),
        in_specs=[
            pl.BlockSpec((tm, D), row),
            part.mod_spec(tm),
            pl.BlockSpec((1, D), const),
            pl.BlockSpec((D, sum(GLA_SEGS)), const),
        ],
        out_specs=[pl.BlockSpec((tm, n), row) for n in GLA_SEGS],
        out_shape=[jax.ShapeDtypeStruct((part.m, n), BF16) for n in GLA_SEGS],
        compiler_params=_cp(1),
        name="gla_in",
    )(x, mod, ng, w)


INTRA_GROUP = 4


def _gla_scan_kernel(*refs, nchunks, hp, has_s0):
    if has_s0:
        q_ref, k_ref, v_ref, a_ref, wa_ref, ba_ref, s0_ref, o_ref, st_scr, qg_scr, kd_scr, dec_scr = refs
        sfin_ref = None
    else:
        q_ref, k_ref, v_ref, a_ref, wa_ref, ba_ref, _, o_ref, sfin_ref, st_scr, qg_scr, kd_scr, dec_scr = refs
    c = CHUNK
    keeps, tris = _tri_masks(c)
    scale = GLA_HK ** -0.5
    was = [wa_ref[d].astype(BF16) for d in (0, 1)]
    bas = [ba_ref[d] for d in (0, 1)]
    width = hp * GLA_HK

    group = INTRA_GROUP if nchunks % INTRA_GROUP == 0 else 2

    def intra(i, carry):
        cis = [i * group + j for j in range(group)]
        sls = [pl.ds(pl.multiple_of(ci * c, c), c) for ci in cis]
        streams = [(j, d) for j in range(group) for d in (0, 1)]
        a = [a_ref[sl, :] for sl in sls]
        pre = {s: _dot(a[s[0]], was[s[1]]) + bas[s[1]] for s in streams}
        split = {}
        for s in streams:
            g = (jnp.minimum(pre[s], 0.0) - jnp.log1p(jnp.exp(-jnp.abs(pre[s])))) * (1.0 / 16.0)
            split[s] = _hi_lo(g)
        b = {s: _dot(tris[s[1]], split[s][0]) + _dot(tris[s[1]], split[s][1]) for s in streams}
        q = [q_ref[sl, :].astype(F32) * scale for sl in sls]
        k = [k_ref[sl, :].astype(F32) for sl in sls]
        qg, kg = {}, {}
        for s in streams:
            j, d = s
            b_last = b[s][c - 1:c, :] if d == 0 else b[s][0:1, :]
            qg[s] = (q[j] * jnp.exp(b[s])).astype(BF16)
            kg[s] = (k[j] * jnp.exp(-b[s])).astype(BF16)
            qg_scr[d, sls[j], :] = qg[s]
            kd_scr[d, sls[j], :] = (k[j] * jnp.exp(b_last - b[s])).astype(BF16)
            dec_scr[d, pl.ds(pl.multiple_of(cis[j] * 8, 8), 8), :] = jnp.broadcast_to(jnp.exp(b_last), (8, width))
        heads = [(s, h) for s in streams for h in range(hp)]
        att = {}
        for s, h in heads:
            ks = slice(h * GLA_HK, (h + 1) * GLA_HK)
            att[s, h] = jnp.where(keeps[s[1]], _dot_nt(qg[s][:, ks], kg[s][:, ks]), 0.0).astype(BF16)
        out = {(s, h): _dot(att[s, h], v_ref[sls[s[0]], h * GLA_HV:(h + 1) * GLA_HV]) for s, h in heads}
        for j in range(group):
            for h in range(hp):
                o_ref[sls[j], h * GLA_HV:(h + 1) * GLA_HV] = out[(j, 0), h] + out[(j, 1), h]
        return carry

    lax.fori_loop(0, nchunks // group, intra, 0)

    for d in (0, 1):
        for h in range(hp):
            st_scr[d, h] = s0_ref[d, h].T if has_s0 else jnp.zeros((GLA_HV, GLA_HK), F32)

    def scan(i, carry):
        cis = (i, nchunks - 1 - i)
        sls = [pl.ds(pl.multiple_of(ci * c, c), c) for ci in cis]
        decs = [dec_scr[d, pl.ds(pl.multiple_of(cis[d] * 8, 8), 1), :] for d in (0, 1)]
        heads = [(d, h) for d in (0, 1) for h in range(hp)]
        ks = lambda h: slice(h * GLA_HK, (h + 1) * GLA_HK)
        vs = lambda h: slice(h * GLA_HV, (h + 1) * GLA_HV)
        st = {dh: st_scr[dh[0], dh[1]] for dh in heads}
        inter = {(d, h): _dot_nt(qg_scr[d, sls[d], ks(h)], st[d, h].astype(BF16)) for d, h in heads}
        grow = {(d, h): _dot_tn(v_ref[sls[d], vs(h)], kd_scr[d, sls[d], ks(h)]) for d, h in heads}
        for d, h in heads:
            o_ref[sls[d], vs(h)] += inter[d, h]
            st_scr[d, h] = st[d, h] * decs[d][:, ks(h)] + grow[d, h]
        return carry

    lax.fori_loop(0, nchunks, scan, 0, unroll=2)
    if sfin_ref is not None:
        for d in (0, 1):
            for h in range(hp):
                sfin_ref[d, h] = st_scr[d, h].T


def _gla_scan(part, q, k, v, a, wa, ba, s0, states):
    t = part.t
    nchunks = t // CHUNK
    assert nchunks % 2 == 0
    hp = GLA_H if t <= 512 else 2
    has_s0 = s0 is not None
    seq_head = lambda b, h: (b, h)
    in_specs = [
        pl.BlockSpec((t, hp * GLA_HK), seq_head),
        pl.BlockSpec((t, hp * GLA_HK), seq_head),
        pl.BlockSpec((t, hp * GLA_HV), seq_head),
        pl.BlockSpec((t, LANES), lambda b, h: (b, 0)),
        pl.BlockSpec((2, LANES, hp * GLA_HK), lambda b, h: (0, 0, h)),
        pl.BlockSpec((2, 1, hp * GLA_HK), lambda b, h: (0, 0, h)),
    ]
    args = [q, k, v, a, wa, ba]
    o_spec = pl.BlockSpec((t, hp * GLA_HV), seq_head)
    o_shape = jax.ShapeDtypeStruct((part.m, GLA_DV), F32)
    state_block = (None, None, 2, hp, GLA_HK, GLA_HV)
    aliases = {}
    if has_s0:
        in_specs.append(pl.BlockSpec(state_block, lambda b, h: (b, states, 0, h, 0, 0)))
        args.append(s0)
        out_specs, out_shape = o_spec, o_shape
    else:
        buf, layer = states
        in_specs.append(pl.BlockSpec(memory_space=pl.ANY))
        args.append(buf)
        aliases = {len(args) - 1: 1}
        out_specs = [o_spec, pl.BlockSpec(state_block, lambda b, h: (b, layer, 0, h, 0, 0))]
        out_shape = [o_shape, jax.ShapeDtypeStruct(buf.shape, F32)]
    res = pl.pallas_call(
        functools.partial(_gla_scan_kernel, nchunks=nchunks, hp=hp, has_s0=has_s0),
        grid=(part.nseq, GLA_H // hp),
        in_specs=in_specs,
        out_specs=out_specs,
        out_shape=out_shape,
        input_output_aliases=aliases,
        scratch_shapes=[
            pltpu.VMEM((2, hp, GLA_HV, GLA_HK), F32),
            pltpu.VMEM((2, t, hp * GLA_HK), BF16),
            pltpu.VMEM((2, t, hp * GLA_HK), BF16),
            pltpu.VMEM((2, nchunks * 8, hp * GLA_HK), F32),
        ],
        compiler_params=_cp(2),
        name="gla_scan",
    )(*args)
    return (res, None) if has_s0 else res


OUT_SUB = 128


def _mix_out_tail(mix, x, mod_ref, ng1_ref, ng2_ref):
    mod = mod_ref[0]
    xn = x + mod[2:3] * _rms(mix, ng1_ref[...])
    h2 = _rms(xn, ng2_ref[...]) * (1.0 + mod[4:5]) + mod[3:4]
    return xn, h2


def _gla_out_kernel(o_ref, r_ref, x_ref, mod_ref, gng_ref, ng1_ref, ng2_ref, w_ref, xn_ref, h2_ref):
    gng = gng_ref[...]
    w = w_ref[...]
    rows = [slice(s * OUT_SUB, (s + 1) * OUT_SUB) for s in range(o_ref.shape[0] // OUT_SUB)]
    gated = []
    for r in rows:
        o = o_ref[r, :]
        on = jnp.concatenate([_rms(o[:, h * GLA_HV:(h + 1) * GLA_HV], gng) for h in range(GLA_H)], axis=1)
        gated.append((on * _silu(r_ref[r, :].astype(F32))).astype(BF16))
    mixes = [_dot(g, w) for g in gated]
    for mix, r in zip(mixes, rows):
        xn, h2 = _mix_out_tail(mix, x_ref[r, :], mod_ref, ng1_ref, ng2_ref)
        xn_ref[r, :] = xn
        h2_ref[r, :] = h2.astype(h2_ref.dtype)


def _gla_out(part, o, r, x, mod, gng, ng1, ng2, w):
    tm = 512
    row = lambda i: (i, 0)
    const = lambda i: (0, 0)
    return pl.pallas_call(
        _gla_out_kernel,
        grid=(part.m // tm,),
        in_specs=[
            pl.BlockSpec((tm, GLA_DV), row),
            pl.BlockSpec((tm, GLA_DV), row),
            pl.BlockSpec((tm, D), row),
            part.mod_spec(tm),
            pl.BlockSpec((1, GLA_HV), const),
            pl.BlockSpec((1, D), const),
            pl.BlockSpec((1, D), const),
            pl.BlockSpec((GLA_DV, D), const),
        ],
        out_specs=[pl.BlockSpec((tm, D), row), pl.BlockSpec((tm, D), row)],
        out_shape=[jax.ShapeDtypeStruct((part.m, D), F32), jax.ShapeDtypeStruct((part.m, D), BF16)],
        compiler_params=_cp(1),
        name="gla_out",
    )(o, r, x, mod, gng, ng1, ng2, w)


def _gla_layer_mixer(part, x, mod, ng, w_in, w_a2, b_a, gng, w_out, s0, states):
    w_in = _pad_cols(w_in, sum(GLA_SEGS)).astype(BF16)
    wa = jnp.zeros((2, LANES, GLA_DK), F32)
    wa = wa.at[0, :GLA_RANK].set(w_a2[0]).at[1, GLA_RANK:2 * GLA_RANK].set(w_a2[1])
    q, k, v, r, a = _gla_in(part, x, mod, ng[0], w_in)
    o, sfin = _gla_scan(part, q, k, v, a, wa, b_a.reshape(2, 1, GLA_DK), s0, states)
    xn, h2 = _gla_out(part, o, r, x, mod, gng.reshape(1, GLA_HV), ng[1], ng[2], w_out.astype(BF16))
    return xn, h2, sfin


def _ffn_kernel(h_ref, x_ref, mod_ref, ng_ref, wg_ref, wu_ref, wd_ref, o_ref):
    h = h_ref[...]
    hd = (_silu(_dot(h, wg_ref[...])) * _dot(h, wu_ref[...])).astype(BF16)
    ff = _dot(hd, wd_ref[...])
    o_ref[...] = x_ref[...] + mod_ref[0][5:6] * _rms(ff, ng_ref[...])


def _ffn(part, h2, x, mod, ng, wg, wu, wd, layer):
    tm = 256
    row = lambda i: (i, 0)
    const = lambda i: (0, 0)
    weights = lambda i: (layer, 0, 0)
    once = pl.Buffered(1)
    return pl.pallas_call(
        _ffn_kernel,
        grid=(part.m // tm,),
        in_specs=[
            pl.BlockSpec((tm, D), row),
            pl.BlockSpec((tm, D), row),
            part.mod_spec(tm),
            pl.BlockSpec((1, D), const),
            pl.BlockSpec((None, D, FFN_DIM), weights, pipeline_mode=once),
            pl.BlockSpec((None, D, FFN_DIM), weights, pipeline_mode=once),
            pl.BlockSpec((None, FFN_DIM, D), weights, pipeline_mode=once),
        ],
        out_specs=pl.BlockSpec((tm, D), row),
        out_shape=jax.ShapeDtypeStruct((part.m, D), F32),
        compiler_params=_cp(1),
        name="ffn",
    )(h2, x, mod, ng, wg, wu, wd)


SSD_NCOL = 8


def _grid_cols_spec(rows=GRID_W):
    per_b = GRID_W // SSD_NCOL
    return pl.BlockSpec((None, rows, SSD_NCOL, D), lambda i: (i // per_b, 0, i % per_b, 0))


def _read_colmajor(x_ref):
    return jnp.concatenate([x_ref[:, c, :] for c in range(SSD_NCOL)], axis=0)


def _write_colmajor(o_ref, val):
    for c in range(SSD_NCOL):
        o_ref[:, c, :] = val[c * GRID_W:(c + 1) * GRID_W, :].astype(o_ref.dtype)


def _ssd_in_kernel(x_ref, xp_ref, xn_ref, mod_ref, ng_ref, wz_ref, wx_ref, wdt_ref, cw_ref, cb_ref,
                   z_ref, xc_ref, bc_ref, cc_ref, dt_ref, p_scr, *, colmajor, blocks_per_seq, tm):
    i = pl.program_id(0)
    first = (i % blocks_per_seq) == 0
    last = (i % blocks_per_seq) == blocks_per_seq - 1
    if colmajor:
        x = _read_colmajor(x_ref)
        xp = xp_ref[:, SSD_NCOL - 1, :]
        xn = xn_ref[:, 0, :]
    else:
        x, xp, xn = x_ref[...], xp_ref[...], xn_ref[...]
    xcat = jnp.concatenate([xp, x, xn], axis=0)
    h = _modulate(xcat, mod_ref, ng_ref, 0, 1).astype(BF16)
    hm = h[8:8 + tm]
    z_ref[...] = _dot(hm, wz_ref[...]).astype(z_ref.dtype)
    dt_ref[...] = _dot(hm, wdt_ref[...])
    cw = cw_ref[...]
    cb = cb_ref[...]
    wcol = 512
    rb = 128
    for cc in range(SSD_XBC // wcol):
        buf = p_scr.at[cc % 2]
        buf[...] = _dot(h, wx_ref[:, cc * wcol:(cc + 1) * wcol])
        buf[0:8, :] = jnp.where(first, 0.0, buf[0:8, :])
        buf[tm + 8:tm + 16, :] = jnp.where(last, 0.0, buf[tm + 8:tm + 16, :])
        out_ref, col0 = (xc_ref, cc * wcol) if cc < 4 else ((bc_ref, 0) if cc == 4 else (cc_ref, 0))
        for r0 in range(0, tm, rb):
            for l0 in range(0, wcol, LANES):
                src = slice(cc * wcol + l0, cc * wcol + l0 + LANES)
                tile = buf[r0:r0 + rb + 16, l0:l0 + LANES]
                acc = cw[0:1, src] * tile[6:6 + rb]
                for j in range(1, SSD_CW):
                    acc = acc + cw[j:j + 1, src] * tile[6 + j:6 + j + rb]
                y = _silu(acc + cb[:, src])
                out_ref[r0:r0 + rb, col0 + l0:col0 + l0 + LANES] = y.astype(out_ref.dtype)


def _ssd_in(part, x, mod, ng, wz, wx, wdt, cw, cb, colmajor):
    m = part.m
    const = lambda i: (0, 0)
    if colmajor:
        tm = SSD_NCOL * GRID_W
        per_b = GRID_W // SSD_NCOL
        xv = x.reshape(part.nseq, GRID_W, GRID_W, D)
        x_specs = [
            _grid_cols_spec(),
            pl.BlockSpec((None, 8, SSD_NCOL, D), lambda i: (i // per_b, GRID_W // 8 - 1,
                                                            jnp.maximum(i % per_b - 1, 0), 0)),
            pl.BlockSpec((None, 8, SSD_NCOL, D), lambda i: (i // per_b, 0, jnp.minimum(i % per_b + 1, per_b - 1), 0)),
        ]
        xs = (xv, xv, xv)
    else:
        tm = part.t
        nb8 = m // 8
        x_specs = [
            pl.BlockSpec((tm, D), lambda i: (i, 0)),
            pl.BlockSpec((8, D), lambda i: (jnp.maximum(i * (tm // 8) - 1, 0), 0)),
            pl.BlockSpec((8, D), lambda i: (jnp.minimum((i + 1) * (tm // 8), nb8 - 1), 0)),
        ]
        xs = (x, x, x)
    row = lambda i: (i, 0)
    widths = (SSD_DI, SSD_DI, SSD_G * SSD_N, SSD_G * SSD_N, LANES)
    dts = (BF16, BF16, BF16, BF16, F32)
    return pl.pallas_call(
        functools.partial(_ssd_in_kernel, colmajor=colmajor, blocks_per_seq=part.t // tm, tm=tm),
        grid=(m // tm,),
        in_specs=x_specs + [
            part.mod_spec(tm),
            pl.BlockSpec((1, D), const),
            pl.BlockSpec((D, SSD_DI), const),
            pl.BlockSpec((D, SSD_XBC), const),
            pl.BlockSpec((D, LANES), const),
            pl.BlockSpec((SSD_CW, SSD_XBC), const),
            pl.BlockSpec((1, SSD_XBC), const),
        ],
        out_specs=[pl.BlockSpec((tm, w), row) for w in widths],
        out_shape=[jax.ShapeDtypeStruct((m, w), dt) for w, dt in zip(widths, dts)],
        scratch_shapes=[pltpu.VMEM((2, tm + 16, 512), F32)],
        compiler_params=_cp(1),
        name="ssd_in",
    )(*xs, mod, ng, wz, wx, wdt, cw, cb)


SSD_PAIRS = SSD_HPG // 2


def _ssd_scan_kernel(*refs, nchunks, has_s0):
    if has_s0:
        (z_ref, x_ref, b_ref, c_ref, dt_ref, dtb_ref, alog_ref, dsk_ref, ng_ref, s0_ref,
         y_ref, h_scr, yacc_scr) = refs
        sfin_ref = None
    else:
        (z_ref, x_ref, b_ref, c_ref, dt_ref, dtb_ref, alog_ref, dsk_ref, ng_ref, _,
         y_ref, sfin_ref, h_scr, yacc_scr) = refs
    c = CHUNK
    grp = pl.program_id(1)
    shift = (LANES - 2 * SSD_HPG * grp) % LANES
    keeps, tris = _tri_masks(c)
    lane = lax.broadcasted_iota(jnp.int32, (1, LANES), 1)
    low = lane < SSD_P
    m_lo = jnp.where(low, 1.0, 0.0).astype(BF16)
    m_hi = jnp.where(low, 0.0, 1.0).astype(BF16)
    a_row = -jnp.exp(alog_ref[...])
    dtb = dtb_ref[...]
    for d in (0, 1):
        for pp in range(SSD_PAIRS):
            if has_s0:
                h_scr[d, pp] = jnp.concatenate([s0_ref[d, 2 * pp], s0_ref[d, 2 * pp + 1]], axis=1)
            else:
                h_scr[d, pp] = jnp.zeros((SSD_N, 2 * SSD_P), F32)

    def step(i, final):
        dirs = (0, 1)
        cis = (i, nchunks - 1 - i)
        sls = [pl.ds(pl.multiple_of(ci * c, c), c) for ci in cis]
        dt = [_softplus(pltpu.roll(dt_ref[sls[d], :], shift, 1) + dtb) for d in dirs]
        split = [_hi_lo(dt[d] * a_row) for d in dirs]
        bm = [b_ref[sls[d], :] for d in dirs]
        cm = [c_ref[sls[d], :] for d in dirs]
        xb = [x_ref[sls[d], :] for d in dirs]
        cum = [(_dot(tris[d], split[d][0]) + _dot(tris[d], split[d][1])) * LOG2E for d in dirs]
        cb = [_dot_nt(cm[d], bm[d]) for d in dirs]
        log_dt = [jnp.log(dt[d]) * LOG2E for d in dirs]
        total = [cum[0][c - 1:c, :], cum[1][0:1, :]]
        f_in = [jnp.exp2(total[d] - cum[d] + log_dt[d]) for d in dirs]
        e_tot = [jnp.exp2(total[d]) for d in dirs]
        src_t = [(cum[d] - log_dt[d]).T for d in dirs]
        f_t = [f_in[d].T for d in dirs]
        bm_t32 = [bm[d].astype(F32).T for d in dirs]
        cm32 = [cm[d].astype(F32) for d in dirs]
        ys = {}
        for pp in range(SSD_PAIRS):
            for d in dirs:
                xp = xb[d][:, pp * LANES:(pp + 1) * LANES]
                x2 = jnp.concatenate([xp * m_lo, xp * m_hi], axis=0)
                hst = h_scr[d, pp]
                hb = hst.astype(BF16)
                ws, cs, bs = [], [], []
                for s in (0, 1):
                    l = d * SSD_HPG + 2 * pp + s
                    ccol = jnp.broadcast_to(cum[d][:, l:l + 1], (c, c))
                    w = jnp.exp2(jnp.where(keeps[d], ccol - src_t[d][l:l + 1, :], -jnp.inf)) * cb[d]
                    ws.append(w.astype(BF16))
                    cs.append((cm32[d] * jnp.exp2(ccol)).astype(BF16))
                    bs.append((bm_t32[d] * f_t[d][l:l + 1, :]).astype(BF16))
                lhs = jnp.concatenate(ws + cs, axis=1)
                rhs = jnp.concatenate([x2, hb * m_lo, hb * m_hi], axis=0)
                ys[d, pp] = _dot(lhs, rhs)
                l0 = d * SSD_HPG + 2 * pp
                dec = jnp.where(low, e_tot[d][:, l0:l0 + 1], e_tot[d][:, l0 + 1:l0 + 2])
                h_scr[d, pp] = hst * dec + _dot(jnp.concatenate(bs, axis=1), x2)
        for d in dirs:
            y = jnp.concatenate([ys[d, pp] for pp in range(SSD_PAIRS)], axis=1)
            if final:
                y = y + yacc_scr[sls[d], :] + dsk_ref[...] * xb[d].astype(F32)
                y = y * _silu(z_ref[sls[d], :].astype(F32))
                y_ref[sls[d], :] = _rms(y, ng_ref[...]).astype(y_ref.dtype)
            else:
                yacc_scr[sls[d], :] = y

    half = nchunks // 2

    def first_half(i, carry):
        step(i, False)
        return carry

    def second_half(i, carry):
        step(i, True)
        return carry

    unroll = 2 if half % 2 == 0 else 1
    lax.fori_loop(0, half, first_half, 0, unroll=unroll)
    lax.fori_loop(half, nchunks, second_half, 0, unroll=unroll)
    if sfin_ref is not None:
        for d in (0, 1):
            for pp in range(SSD_PAIRS):
                hst = h_scr[d, pp]
                sfin_ref[d, 2 * pp] = hst[:, :SSD_P]
                sfin_ref[d, 2 * pp + 1] = hst[:, SSD_P:]


def _ssd_scan(part, z, xc, bc, cc, dt, dtb, alog, dsk, ng, s0, states):
    t = part.t
    nchunks = t // CHUNK
    assert nchunks % 2 == 0
    has_s0 = s0 is not None
    sg = lambda b, g: (b, g)
    grp_row = lambda b, g: (g, 0, 0)
    in_specs = [
        pl.BlockSpec((t, SSD_GW), sg),
        pl.BlockSpec((t, SSD_GW), sg),
        pl.BlockSpec((t, SSD_N), sg),
        pl.BlockSpec((t, SSD_N), sg),
        pl.BlockSpec((t, LANES), lambda b, g: (b, 0)),
        pl.BlockSpec((None, 1, LANES), grp_row),
        pl.BlockSpec((None, 1, LANES), grp_row),
        pl.BlockSpec((None, 1, SSD_GW), grp_row),
        pl.BlockSpec((None, 1, SSD_GW), grp_row),
    ]
    args = [z, xc, bc, cc, dt, dtb, alog, dsk, ng]
    y_spec = pl.BlockSpec((t, SSD_GW), sg)
    y_shape = jax.ShapeDtypeStruct((part.m, SSD_DI), BF16)
    state_block = (None, None, 2, SSD_HPG, SSD_N, SSD_P)
    aliases = {}
    if has_s0:
        in_specs.append(pl.BlockSpec(state_block, lambda b, g: (b, states, 0, g, 0, 0)))
        args.append(s0)
        out_specs, out_shape = y_spec, y_shape
    else:
        buf, layer = states
        in_specs.append(pl.BlockSpec(memory_space=pl.ANY))
        args.append(buf)
        aliases = {len(args) - 1: 1}
        out_specs = [y_spec, pl.BlockSpec(state_block, lambda b, g: (b, layer, 0, g, 0, 0))]
        out_shape = [y_shape, jax.ShapeDtypeStruct(buf.shape, F32)]
    res = pl.pallas_call(
        functools.partial(_ssd_scan_kernel, nchunks=nchunks, has_s0=has_s0),
        grid=(part.nseq, SSD_G),
        in_specs=in_specs,
        out_specs=out_specs,
        out_shape=out_shape,
        input_output_aliases=aliases,
        scratch_shapes=[pltpu.VMEM((2, SSD_PAIRS, SSD_N, 2 * SSD_P), F32), pltpu.VMEM((t, SSD_GW), F32)],
        compiler_params=_cp(2),
        name="ssd_scan",
    )(*args)
    return (res, None) if has_s0 else res


def _ssd_out_kernel(y_ref, x_ref, mod_ref, ng1_ref, ng2_ref, w_ref, xn_ref, h2_ref, *, colmajor):
    x = _read_colmajor(x_ref) if colmajor else x_ref[...]
    w = w_ref[...]
    rows = [slice(s * OUT_SUB, (s + 1) * OUT_SUB) for s in range(x.shape[0] // OUT_SUB)]
    mixes = [_dot(y_ref[r, :], w) for r in rows]
    tails = [_mix_out_tail(mix, x[r], mod_ref, ng1_ref, ng2_ref) for mix, r in zip(mixes, rows)]
    xn = jnp.concatenate([t[0] for t in tails], axis=0)
    h2 = jnp.concatenate([t[1] for t in tails], axis=0)
    if colmajor:
        _write_colmajor(xn_ref, xn)
        _write_colmajor(h2_ref, h2)
    else:
        xn_ref[...] = xn
        h2_ref[...] = h2


def _ssd_out(part, y, x, mod, ng1, ng2, w, colmajor):
    m = part.m
    const = lambda i: (0, 0)
    row = lambda i: (i, 0)
    if colmajor:
        tm = SSD_NCOL * GRID_W
        vshape = (part.nseq, GRID_W, GRID_W, D)
        xspec = _grid_cols_spec()
        x = x.reshape(vshape)
    else:
        tm = 512
        vshape = (m, D)
        xspec = pl.BlockSpec((tm, D), row)
    xn, h2 = pl.pallas_call(
        functools.partial(_ssd_out_kernel, colmajor=colmajor),
        grid=(m // tm,),
        in_specs=[
            pl.BlockSpec((tm, SSD_DI), row),
            xspec,
            part.mod_spec(tm),
            pl.BlockSpec((1, D), const),
            pl.BlockSpec((1, D), const),
            pl.BlockSpec((SSD_DI, D), const),
        ],
        out_specs=[xspec, xspec],
        out_shape=[jax.ShapeDtypeStruct(vshape, F32), jax.ShapeDtypeStruct(vshape, F32)],
        compiler_params=_cp(1),
        name="ssd_out",
    )(y, x, mod, ng1, ng2, w)
    return xn.reshape(m, D), h2.reshape(m, D)


def _ssd_layer_mixer(part, x, mod, ng, w_in, conv_w, conv_b, dt_bias, a_log, d_skip, norm_g, w_out, s0, states,
                     colmajor):
    wz = w_in[:, :SSD_DI].astype(BF16)
    wx = w_in[:, SSD_DI:SSD_DI + SSD_XBC].astype(BF16)
    wdt = w_in[:, SSD_DI + SSD_XBC:].reshape(D, 2, SSD_G, SSD_HPG).transpose(0, 2, 1, 3)
    wdt = _pad_cols(wdt.reshape(D, 2 * SSD_NH), LANES).astype(BF16)
    per_grp = lambda p: _pad_cols(p.reshape(2, SSD_G, SSD_HPG).transpose(1, 0, 2).reshape(SSD_G, 2 * SSD_HPG),
                                  LANES).reshape(SSD_G, 1, LANES)
    dsk = jnp.repeat(d_skip, SSD_P).reshape(SSD_G, 1, SSD_GW)
    sng = norm_g.reshape(SSD_G, 1, SSD_GW)
    z, xc, bc, cc, dt = _ssd_in(part, x, mod, ng[0], wz, wx, wdt, conv_w, conv_b.reshape(1, SSD_XBC), colmajor)
    y, sfin = _ssd_scan(part, z, xc, bc, cc, dt, per_grp(dt_bias), per_grp(a_log), dsk, sng, s0, states)
    xn, h2 = _ssd_out(part, y, x, mod, ng[1], ng[2], w_out.astype(BF16), colmajor)
    return xn, h2, sfin


ROUTER_TM = 512
R_E0, R_E1, R_G0, R_G1, R_RANK0, R_RANK1 = range(6)


def _router_kernel(h_ref, w_ref, r_ref, cnt_ref, carry_scr):
    tm = ROUTER_TM

    @pl.when(pl.program_id(0) == 0)
    def _():
        carry_scr[...] = jnp.zeros_like(carry_scr)

    lane = lax.broadcasted_iota(jnp.int32, (tm, LANES), 1).astype(F32)
    logits = jnp.where(lane < N_EXP, _dot(h_ref[...].astype(BF16), w_ref[...]), -jnp.inf)
    m0 = jnp.max(logits, axis=1, keepdims=True)
    i0 = jnp.min(jnp.where(logits == m0, lane, float(LANES)), axis=1, keepdims=True)
    pick0 = lane == i0
    rest = jnp.where(pick0, -jnp.inf, logits)
    m1 = jnp.max(rest, axis=1, keepdims=True)
    i1 = jnp.min(jnp.where(rest == m1, lane, float(LANES)), axis=1, keepdims=True)
    pick1 = lane == i1
    e1 = jnp.exp(m1 - m0)
    den = 1.0 + e1
    oh0 = jnp.where(pick0, 1.0, 0.0)
    oh1 = jnp.where(pick1, 1.0, 0.0)
    oh = oh0 + oh1
    r_i = lax.broadcasted_iota(jnp.int32, (tm, tm), 0)
    c_i = lax.broadcasted_iota(jnp.int32, (tm, tm), 1)
    before = jnp.where(r_i > c_i, 1.0, 0.0).astype(BF16)
    base = _dot(before, oh.astype(BF16)) + carry_scr[...]
    rank0 = jnp.sum(base * oh0, axis=1, keepdims=True)
    rank1 = jnp.sum(base * oh1, axis=1, keepdims=True)
    carry_scr[...] += jnp.sum(oh, axis=0, keepdims=True)
    out = jnp.zeros((tm, LANES), F32)
    for idx, val in ((R_E0, i0), (R_E1, i1), (R_G0, 1.0 / den), (R_G1, e1 / den),
                     (R_RANK0, rank0), (R_RANK1, rank1)):
        out = jnp.where(lane == idx, val, out)
    r_ref[...] = out
    cnt_ref[...] = jnp.broadcast_to(carry_scr[...], (8, LANES))


def _router(part, h2, w):
    tm = ROUTER_TM
    return pl.pallas_call(
        _router_kernel,
        grid=(part.m // tm,),
        in_specs=[pl.BlockSpec((tm, D), lambda i: (i, 0)), pl.BlockSpec((D, LANES), lambda i: (0, 0))],
        out_specs=[pl.BlockSpec((tm, LANES), lambda i: (i, 0)), pl.BlockSpec((8, LANES), lambda i: (0, 0))],
        out_shape=[jax.ShapeDtypeStruct((part.m, LANES), F32), jax.ShapeDtypeStruct((8, LANES), F32)],
        scratch_shapes=[pltpu.VMEM((1, LANES), F32)],
        compiler_params=_cp(1),
        name="router",
    )(h2, w)


MOE_TM = 256
DMA_UNROLL = 8


def _row_copy(src_ref, src_row, dst_ref, dst_row, sem):
    return pltpu.make_async_copy(src_ref.at[pl.ds(src_row, 1), :], dst_ref.at[pl.ds(dst_row, 1), :], sem)


def _for_rows(n, fn):
    def body(g, carry):
        for u in range(DMA_UNROLL):
            fn(g * DMA_UNROLL + u, u % 2)
        return carry

    lax.fori_loop(0, n // DMA_UNROLL, body, 0)


def _dispatch_kernel(cnt_ref, pad_ref, start_ref, d0_ref, d1_ref, h_ref, slots_ref, zero_scr, sem, zsem):
    tm = MOE_TM

    def start(r, prio):
        _row_copy(h_ref, r, slots_ref, d0_ref[0, 0, r], sem).start(priority=prio)
        _row_copy(h_ref, r, slots_ref, d1_ref[0, 0, r], sem).start(priority=prio)

    def wait(r, prio):
        _row_copy(h_ref, 0, slots_ref, 0, sem).wait()
        _row_copy(h_ref, 0, slots_ref, 0, sem).wait()

    _for_rows(tm, start)

    @pl.when(pl.program_id(0) == 0)
    def _():
        zero_scr[...] = jnp.zeros_like(zero_scr)
        for e in range(N_EXP + 1):
            first = start_ref[e]

            def fill(r, carry, first=first):
                _row_copy(zero_scr, 0, slots_ref, first + r, zsem).start()
                return carry

            def fill_wait(r, carry):
                _row_copy(zero_scr, 0, slots_ref, 0, zsem).wait()
                return carry

            lax.fori_loop(cnt_ref[e], pad_ref[e], fill, 0)
            lax.fori_loop(cnt_ref[e], pad_ref[e], fill_wait, 0)

    _for_rows(tm, wait)


def _dispatch(part, h2, dest0, dest1, counts, padded, pad_starts, n_slots):
    tm = MOE_TM
    nb = part.m // tm
    idx_spec = pl.BlockSpec((1, 1, tm), lambda i, *_: (i, 0, 0), memory_space=pltpu.SMEM)
    grid_spec = pltpu.PrefetchScalarGridSpec(
        num_scalar_prefetch=3,
        grid=(nb,),
        in_specs=[idx_spec, idx_spec, pl.BlockSpec((tm, D), lambda i, *_: (i, 0))],
        out_specs=pl.BlockSpec(memory_space=pl.ANY),
        scratch_shapes=[pltpu.VMEM((8, D), F32), pltpu.SemaphoreType.DMA(()), pltpu.SemaphoreType.DMA(())],
    )
    return pl.pallas_call(
        _dispatch_kernel,
        grid_spec=grid_spec,
        out_shape=jax.ShapeDtypeStruct((n_slots, D), F32),
        compiler_params=_cp(1),
        name="moe_dispatch",
    )(counts, padded, pad_starts, dest0.reshape(nb, 1, tm), dest1.reshape(nb, 1, tm), h2)


def _expert_kernel(be_ref, nv_ref, xs_ref, wg_ref, wu_ref, wd_ref, ys_ref):
    del be_ref
    used = pl.program_id(0) < nv_ref[0]

    @pl.when(jnp.logical_not(used))
    def _():
        ys_ref[...] = jnp.zeros_like(ys_ref)

    @pl.when(used)
    def _():
        x = xs_ref[...].astype(BF16)
        acc = None
        for f in range(EXP_DIM // EXP_FCHUNK):
            fs = slice(f * EXP_FCHUNK, (f + 1) * EXP_FCHUNK)
            hd = (_silu(_dot(x, wg_ref[:, fs])) * _dot(x, wu_ref[:, fs])).astype(BF16)
            part = _dot(hd, wd_ref[fs, :])
            acc = part if acc is None else acc + part
        ys_ref[...] = acc


def _experts(slots, blk_e, n_valid, wg, wu, wd, layer):
    n_blocks = slots.shape[0] // EXP_ROWS
    rows = lambda j, be, nv: (j, 0)
    expert = lambda j, be, nv: (layer, be[j], 0, 0)
    grid_spec = pltpu.PrefetchScalarGridSpec(
        num_scalar_prefetch=2,
        grid=(n_blocks,),
        in_specs=[
            pl.BlockSpec((EXP_ROWS, D), rows),
            pl.BlockSpec((None, None, D, EXP_DIM), expert),
            pl.BlockSpec((None, None, D, EXP_DIM), expert),
            pl.BlockSpec((None, None, EXP_DIM, D), expert),
        ],
        out_specs=pl.BlockSpec((EXP_ROWS, D), rows),
    )
    return pl.pallas_call(
        _expert_kernel,
        grid_spec=grid_spec,
        out_shape=jax.ShapeDtypeStruct(slots.shape, F32),
        compiler_params=_cp(1, 60 * 1024 * 1024),
        name="moe_experts",
    )(blk_e, n_valid, slots, wg, wu, wd)


def _combine_kernel(d0_ref, d1_ref, n0_ref, n1_ref, r_ref, x_ref, mod_ref, ng_ref, ys_ref, o_ref,
                    y0_scr, y1_scr, sem):
    tm = MOE_TM
    i = pl.program_id(0)
    nb = pl.num_programs(0)
    slot = i % 2

    def gather(i0_ref, i1_ref, s):
        def start(r, prio):
            _row_copy(ys_ref, i0_ref[0, 0, r], y0_scr.at[s], r, sem.at[s]).start(priority=prio)
            _row_copy(ys_ref, i1_ref[0, 0, r], y1_scr.at[s], r, sem.at[s]).start(priority=prio)

        _for_rows(tm, start)

    @pl.when(i == 0)
    def _():
        gather(d0_ref, d1_ref, 0)

    @pl.when(i + 1 < nb)
    def _():
        gather(n0_ref, n1_ref, 1 - slot)

    def wait(r, prio):
        _row_copy(ys_ref, 0, y0_scr.at[slot], 0, sem.at[slot]).wait()
        _row_copy(ys_ref, 0, y1_scr.at[slot], 0, sem.at[slot]).wait()

    _for_rows(tm, wait)
    rt = r_ref[...]
    ff = rt[:, R_G0:R_G0 + 1] * y0_scr[slot] + rt[:, R_G1:R_G1 + 1] * y1_scr[slot]
    o_ref[...] = x_ref[...] + mod_ref[0][5:6] * _rms(ff, ng_ref[...])


def _combine(part, ys, dest0, dest1, routes, x, mod, ng):
    tm = MOE_TM
    nb = part.m // tm
    idx_spec = pl.BlockSpec((1, 1, tm), lambda i: (i, 0, 0), memory_space=pltpu.SMEM)
    nxt_spec = pl.BlockSpec((1, 1, tm), lambda i: (jnp.minimum(i + 1, nb - 1), 0, 0), memory_space=pltpu.SMEM)
    row = lambda i: (i, 0)
    d0 = dest0.reshape(nb, 1, tm)
    d1 = dest1.reshape(nb, 1, tm)
    return pl.pallas_call(
        _combine_kernel,
        grid=(nb,),
        in_specs=[
            idx_spec, idx_spec, nxt_spec, nxt_spec,
            pl.BlockSpec((tm, LANES), row),
            pl.BlockSpec((tm, D), row),
            part.mod_spec(tm),
            pl.BlockSpec((1, D), lambda i: (0, 0)),
            pl.BlockSpec(memory_space=pl.ANY),
        ],
        out_specs=pl.BlockSpec((tm, D), row),
        out_shape=jax.ShapeDtypeStruct((part.m, D), F32),
        scratch_shapes=[pltpu.VMEM((2, tm, D), F32), pltpu.VMEM((2, tm, D), F32), pltpu.SemaphoreType.DMA((2,))],
        compiler_params=_cp(1),
        name="moe_combine",
    )(d0, d1, d0, d1, routes, x, mod, ng, ys)


def _moe(part, h2, x, mod, ng, w_router, wg, wu, wd, layer):
    routes, cnt = _router(part, h2, w_router)
    n_asg = 2 * part.m
    n_blocks = n_asg // EXP_ROWS + N_EXP
    counts = cnt[0, :N_EXP].astype(jnp.int32)
    padded = (counts + EXP_ROWS - 1) // EXP_ROWS * EXP_ROWS
    pad_ends = jnp.cumsum(padded)
    pad_starts = pad_ends - padded
    e0 = routes[:, R_E0].astype(jnp.int32)
    e1 = routes[:, R_E1].astype(jnp.int32)
    dest0 = pad_starts[e0] + routes[:, R_RANK0].astype(jnp.int32)
    dest1 = pad_starts[e1] + routes[:, R_RANK1].astype(jnp.int32)
    n_valid = pad_ends[-1] // EXP_ROWS
    blk = jnp.arange(n_blocks, dtype=jnp.int32)
    blk_e = jnp.sum(blk[:, None] * EXP_ROWS >= pad_ends[None, :], axis=1).astype(jnp.int32)
    blk_e = jnp.minimum(blk_e, N_EXP - 1)
    blk_e = jnp.where(blk < n_valid, blk_e, blk_e[jnp.maximum(n_valid - 1, 0)])
    n_slots = n_blocks * EXP_ROWS
    tail = lambda v, last: jnp.concatenate([v, jnp.reshape(last, (1,))]).astype(jnp.int32)
    slots = _dispatch(part, h2, dest0, dest1, tail(counts, 0), tail(padded, n_slots - pad_ends[-1]),
                      tail(pad_starts, pad_ends[-1]), n_slots)
    ys = _experts(slots, blk_e, n_valid.reshape(1).astype(jnp.int32), wg, wu, wd, layer)
    return _combine(part, ys, dest0, dest1, routes, x, mod, ng)


def _pad_cols(w, n):
    return jnp.pad(w, ((0, 0), (0, n - w.shape[1])))


def kernel(x_prompt, x_sample, state_gla, state_ssd, c, c_ctx, w_mod, b_mod, norm_g, gla_w_in, gla_w_a2, gla_b_a,
           gla_norm_g, gla_w_out, ssd_w_in, ssd_conv_w, ssd_conv_b, ssd_dt_bias, ssd_a_log, ssd_d, ssd_norm_g,
           ssd_w_out, ffn_w_gate, ffn_w_up, ffn_w_down, moe_w_router, moe_w_gate, moe_w_up, moe_w_down):
    bp, seq, _ = x_prompt.shape
    bd, dseq, _ = x_sample.shape
    parts = (_Part(bp, seq, 0, False), _Part(bd, dseq, 1, True))
    xs = [x_prompt.reshape(bp * seq, D), x_sample.reshape(bd * dseq, D)]

    c_all = jnp.concatenate([c_ctx[None], c, jnp.zeros((8 - 1 - bd, D), F32)], axis=0)
    mods = _modulation(c_all, w_mod, b_mod).reshape(DEPTH, 8, 6, D)

    ffn_w = [w.astype(BF16) for w in (ffn_w_gate, ffn_w_up, ffn_w_down)]
    moe_w = [w.astype(BF16) for w in (moe_w_gate, moe_w_up, moe_w_down)]
    gla_states = jnp.zeros((bp,) + state_gla.shape[1:], F32)
    ssd_states = jnp.zeros((bp,) + state_ssd.shape[1:], F32)
    for l in range(DEPTH):
        j = l // 2
        mod = mods[l]
        ng = [norm_g[l, i].reshape(1, D) for i in range(4)]
        if l % 2 == 0:
            for pi, part in enumerate(parts):
                s0, states = (state_gla, j) if pi == 1 else (None, (gla_states, j))
                xn, h2, sfin = _gla_layer_mixer(part, xs[pi], mod, ng, gla_w_in[j], gla_w_a2[j], gla_b_a[j],
                                                gla_norm_g[j], gla_w_out[j], s0, states)
                if pi == 0:
                    gla_states = sfin
                xs[pi] = _ffn(part, h2, xn, mod, ng[3], *ffn_w, j)
        else:
            w_router = _pad_cols(moe_w_router[j], LANES).astype(BF16)
            for pi, part in enumerate(parts):
                s0, states = (state_ssd, j) if pi == 1 else (None, (ssd_states, j))
                xn, h2, sfin = _ssd_layer_mixer(part, xs[pi], mod, ng, ssd_w_in[j], ssd_conv_w[j], ssd_conv_b[j],
                                                ssd_dt_bias[j], ssd_a_log[j], ssd_d[j], ssd_norm_g[j], ssd_w_out[j],
                                                s0, states, pi == 1)
                if pi == 0:
                    ssd_states = sfin
                xs[pi] = _moe(part, h2, xn, mod, ng[3], w_router, *moe_w, j)

    y_prompt = xs[0].reshape(bp, seq, D)
    y_sample = xs[1].reshape(bd, dseq, D)
    return y_prompt, y_sample, gla_states, ssd_states
```
